```python
import jax
import jax.numpy as jnp
from jax import lax
import numpy as np

D_MODEL = 4096
BATCH = 32
SEQ = 256
DEPTH = 1
DEC_BATCH = 4
DEC_SEQ = 2048
PAST_LEN = 256

GRID_W = 64
D_ATT = D_MODEL // 2
D_FOURIER = D_MODEL - D_ATT
D_MIX = D_ATT + D_FOURIER
HEAD_DIM = 128
N_HEADS = D_ATT // HEAD_DIM
N_FGROUPS = 4
FGROUP_DIM = D_FOURIER // N_FGROUPS
WIN_H = 8
WIN_W = 16
D_FF = ((8 * D_MODEL + 3 * 256 - 1) // (3 * 256)) * 256
Q_BLOCK = 128
EPS = 1e-6
NEG_INF = -1e30

kernel_name = "hybrid_natten_fnet_dit_step"


def rmsnorm(x, g):
    xf = x.astype(jnp.float32)
    y = xf * lax.rsqrt(jnp.mean(xf * xf, axis=-1, keepdims=True) + EPS)
    return (y * g.astype(jnp.float32)).astype(x.dtype)


def ada_mod(cvec, w_ada, b_ada):
    m = jax.nn.silu(cvec) @ w_ada + b_ada
    return jnp.split(m[:, None, :], 6, axis=-1)


def modulate(h, shift, scale):
    return h * (1 + scale) + shift


def swiglu(h, w_gate, w_up, w_down):
    return (jax.nn.silu(h @ w_gate) * (h @ w_up)) @ w_down


def mixer_split(h, w_in):
    b, t, _ = h.shape
    proj = h @ w_in
    q, k, v, u = jnp.split(proj, [D_ATT, 2 * D_ATT, 3 * D_ATT], axis=-1)
    return (q.reshape(b, t, N_HEADS, HEAD_DIM), k.reshape(b, t, N_HEADS, HEAD_DIM),
            v.reshape(b, t, N_HEADS, HEAD_DIM), u)


def fourier_mix(u):
    b, t, _ = u.shape
    ug = u.reshape(b, t, N_FGROUPS, FGROUP_DIM).astype(jnp.float32)
    f = jnp.fft.fft2(ug, axes=(1, 3), norm="ortho").real
    return f.reshape(b, t, D_FOURIER).astype(u.dtype)


def _dense_attend(q, k, v):
    s = jnp.einsum("bqhd,bkhd->bhqk", q, k).astype(jnp.float32) * (HEAD_DIM ** -0.5)
    p = jax.nn.softmax(s, axis=-1).astype(v.dtype)
    return jnp.einsum("bhqk,bkhd->bqhd", p, v)


def context_attention(q, k, v):
    b, s, h, d = q.shape
    qb = q.reshape(b, s // Q_BLOCK, Q_BLOCK, h, d).transpose(1, 0, 2, 3, 4)
    o = lax.map(lambda qi: _dense_attend(qi, k, v), qb)
    return o.transpose(1, 0, 2, 3, 4).reshape(b, s, h * d)


def neighbourhood_attention(q, k, v, k_ctx, v_ctx, rpb):
    b, t, h, d = q.shape
    rows = t // GRID_W
    win_h = min(WIN_H, rows)
    n_loc = win_h * GRID_W
    col = np.arange(GRID_W)
    cstart = np.clip(col - WIN_W // 2, 0, GRID_W - WIN_W)
    col_mask = (col[None, :] >= cstart[:, None]) & (col[None, :] < cstart[:, None] + WIN_W)
    dc_idx = jnp.asarray(np.clip(col[None, :] - col[:, None] + WIN_W - 1, 0, 2 * WIN_W - 2))
    mask = jnp.asarray(np.tile(col_mask, (1, win_h)))
    r = np.arange(rows)
    rstart = np.clip(r - WIN_H // 2, 0, rows - win_h)
    key_rows = rstart[:, None] + np.arange(win_h)[None, :]
    dr_idx = key_rows - r[:, None] + WIN_H - 1
    qg = q.reshape(b, rows, GRID_W, h, d).transpose(1, 0, 2, 3, 4)
    kg = k.reshape(b, rows, GRID_W, h, d)
    vg = v.reshape(b, rows, GRID_W, h, d)
    scale = d ** -0.5

    def row_block(args):
        q_r, rows_r, dr_r = args
        k_win = jnp.take(kg, rows_r, axis=1).reshape(b, n_loc, h, d)
        v_win = jnp.take(vg, rows_r, axis=1).reshape(b, n_loc, h, d)
        bias = rpb[:, dr_r[:, None, None], dc_idx[None, :, :]]
        bias = bias.transpose(0, 2, 1, 3).reshape(h, GRID_W, n_loc).astype(jnp.float32)
        s_loc = jnp.einsum("bqhd,bkhd->bhqk", q_r, k_win).astype(jnp.float32) * scale + bias
        s_loc = jnp.where(mask, s_loc, NEG_INF)
        s_ctx = jnp.einsum("bqhd,bkhd->bhqk", q_r, k_ctx).astype(jnp.float32) * scale
        p = jax.nn.softmax(jnp.concatenate([s_loc, s_ctx], axis=-1), axis=-1).astype(v.dtype)
        return (jnp.einsum("bhqk,bkhd->bqhd", p[..., :n_loc], v_win)
                + jnp.einsum("bhqk,bkhd->bqhd", p[..., n_loc:], v_ctx))

    o = lax.map(row_block, (qg, jnp.asarray(key_rows), jnp.asarray(dr_idx)))
    return o.transpose(1, 0, 2, 3, 4).reshape(b, t, h * d)


def setup_inputs(seed: int = 0) -> dict:
    key = jax.random.key(seed)
    ks = jax.random.split(key, 17)
    f32 = jnp.float32

    def nrm(k, shape, s=1.0):
        return jax.random.normal(k, shape, f32) * s

    return {
        "x_prompt": nrm(ks[0], (BATCH, SEQ, D_MODEL)),
        "x_sample": nrm(ks[1], (DEC_BATCH, DEC_SEQ, D_MODEL)),
        "cache_k": nrm(ks[2], (DEC_BATCH, DEPTH, PAST_LEN, N_HEADS, HEAD_DIM)),
        "cache_v": nrm(ks[3], (DEC_BATCH, DEPTH, PAST_LEN, N_HEADS, HEAD_DIM)),
        "c": nrm(ks[4], (DEC_BATCH, D_MODEL)),
        "c_ctx": nrm(ks[5], (D_MODEL,)),
        "w_ada": nrm(ks[6], (DEPTH, D_MODEL, 6 * D_MODEL), 0.5 * D_MODEL ** -0.5),
        "b_ada": nrm(ks[7], (DEPTH, 6 * D_MODEL), 0.01),
        "norm1_g": 1.0 + nrm(ks[8], (DEPTH, D_MODEL), 0.02),
        "w_in": nrm(ks[9], (DEPTH, D_MODEL, 3 * D_ATT + D_FOURIER), D_MODEL ** -0.5),
        "rpb": nrm(ks[10], (DEPTH, N_HEADS, 2 * WIN_H - 1, 2 * WIN_W - 1), 0.1),
        "w_out": nrm(ks[11], (DEPTH, D_MIX, D_MODEL), D_MIX ** -0.5),
        "norm2_g": 1.0 + nrm(ks[12], (DEPTH, D_MODEL), 0.02),
        "w_gate": nrm(ks[13], (DEPTH, D_MODEL, D_FF), D_MODEL ** -0.5),
        "w_up": nrm(ks[14], (DEPTH, D_MODEL, D_FF), D_MODEL ** -0.5),
        "w_down": nrm(ks[15], (DEPTH, D_FF, D_MODEL), D_FF ** -0.5),
        "final_g": 1.0 + nrm(ks[16], (D_MODEL,), 0.02),
    }


def reference(x_prompt, x_sample, cache_k, cache_v, c, c_ctx, w_ada, b_ada, norm1_g, w_in,
              rpb, w_out, norm2_g, w_gate, w_up, w_down, final_g):
    xp = x_prompt
    xs = x_sample
    new_k = []
    new_v = []
    for l in range(DEPTH):
        sh1, sc1, g1, sh2, sc2, g2 = ada_mod(c_ctx[None, :], w_ada[l], b_ada[l])
        h = modulate(rmsnorm(xp, norm1_g[l]), sh1, sc1)
        q, k, v, u = mixer_split(h, w_in[l])
        mix = jnp.concatenate([context_attention(q, k, v), fourier_mix(u)], axis=-1)
        xp = xp + g1 * (mix @ w_out[l])
        h = modulate(rmsnorm(xp, norm2_g[l]), sh2, sc2)
        xp = xp + g2 * swiglu(h, w_gate[l], w_up[l], w_down[l])
        new_k.append(k)
        new_v.append(v)
        sh1, sc1, g1, sh2, sc2, g2 = ada_mod(c, w_ada[l], b_ada[l])
        h = modulate(rmsnorm(xs, norm1_g[l]), sh1, sc1)
        q, k, v, u = mixer_split(h, w_in[l])
        att = neighbourhood_attention(q, k, v, cache_k[:, l], cache_v[:, l], rpb[l])
        mix = jnp.concatenate([att, fourier_mix(u)], axis=-1)
        xs = xs + g1 * (mix @ w_out[l])
        h = modulate(rmsnorm(xs, norm2_g[l]), sh2, sc2)
        xs = xs + g2 * swiglu(h, w_gate[l], w_up[l], w_down[l])
    y_prompt = rmsnorm(xp, final_g)
    y_sample = rmsnorm(xs, final_g)
    new_cache_k = jnp.stack(new_k, axis=1)
    new_cache_v = jnp.stack(new_v, axis=1)
    return (y_prompt, y_sample, new_cache_k, new_cache_v)
```

```python
import functools

import numpy as np
import jax
import jax.numpy as jnp
from jax import lax
from jax.experimental import pallas as pl
from jax.experimental.pallas import tpu as pltpu

F32 = jnp.float32
BF16 = jnp.bfloat16

D_MODEL = 4096
BATCH = 32
SEQ = 256
DEC_BATCH = 4
DEC_SEQ = 2048
PAST_LEN = 256
GRID_W = 64
GRID_ROWS = DEC_SEQ // GRID_W
D_ATT = 2048
D_FOURIER = 2048
HEAD_DIM = 128
N_HEADS = 16
N_FGROUPS = 4
FGROUP_DIM = 512
WIN_H = 8
WIN_W = 16
D_FF = 11008
D_FF_PAD = 11264
EPS = 1e-6
NEG_INF = -1e30
SCALE = HEAD_DIM ** -0.5

NP_TOK = BATCH * SEQ
NS_TOK = DEC_BATCH * DEC_SEQ
N_TOK = NP_TOK + NS_TOK
N_MOD = 8
N_LOC = WIN_H * GRID_W
N_RCLASS = 8

VMEM_LIMIT = 56 * 1024 * 1024


def _params(n_axes, vmem=VMEM_LIMIT):
    return pltpu.CompilerParams(dimension_semantics=("arbitrary",) * n_axes,
                                vmem_limit_bytes=vmem)


def _mod_row(i, tm):
    nh = NP_TOK // tm
    return jnp.where(i < nh, 0, 1 + (i - nh) // (DEC_SEQ // tm))


def _ada_kernel(c_ref, w_ref, b_ref, o_ref):
    c = c_ref[...]
    s = (c * jax.nn.sigmoid(c)).astype(BF16)
    o_ref[...] = jnp.dot(s, w_ref[...].astype(BF16), preferred_element_type=F32) + b_ref[...]


def _ada(cvec, w_ada, b_ada):
    tn = 512
    n = w_ada.shape[1]
    return pl.pallas_call(
        _ada_kernel,
        grid=(n // tn,),
        in_specs=[pl.BlockSpec((N_MOD, D_MODEL), lambda j: (0, 0)),
                  pl.BlockSpec((D_MODEL, tn), lambda j: (0, j)),
                  pl.BlockSpec((1, tn), lambda j: (0, j))],
        out_specs=pl.BlockSpec((N_MOD, tn), lambda j: (0, j)),
        out_shape=jax.ShapeDtypeStruct((N_MOD, n), F32),
        compiler_params=_params(1),
        name="ada_mod",
    )(cvec, w_ada, b_ada)


ROW_CHUNK = 16


def _row_chunks(n_rows, fn):
    def body(c, carry):
        fn(pl.ds(pl.multiple_of(c * ROW_CHUNK, ROW_CHUNK), ROW_CHUNK))
        return carry
    lax.fori_loop(0, n_rows // ROW_CHUNK, body, 0)


def _norm_mod(x_ref, g_ref, sh_ref, sc_ref, o_ref):
    def chunk(rows):
        x = x_ref[rows, :]
        ms = jnp.mean(x * x, axis=-1, keepdims=True)
        y = x * lax.rsqrt(ms + EPS) * g_ref[...]
        o_ref[rows, :] = (y * (1.0 + sc_ref[0]) + sh_ref[0]).astype(o_ref.dtype)
    _row_chunks(o_ref.shape[0], chunk)


def _norm_mod2_kernel(xp_ref, xs_ref, g_ref, sh_ref, sc_ref, o_ref, *, nh):
    i = pl.program_id(0)

    @pl.when(i < nh)
    def _():
        _norm_mod(xp_ref, g_ref, sh_ref, sc_ref, o_ref)

    @pl.when(i >= nh)
    def _():
        _norm_mod(xs_ref, g_ref, sh_ref, sc_ref, o_ref)


def _norm_mod1_kernel(x_ref, g_ref, sh_ref, sc_ref, o_ref):
    _norm_mod(x_ref, g_ref, sh_ref, sc_ref, o_ref)


def _mod_spec(part, tm):
    return pl.BlockSpec((1, 1, D_MODEL), lambda i: (_mod_row(i, tm) * 6 + part, 0, 0))


def _norm1(xp, xs, g, mod3):
    tm = 256
    nh = NP_TOK // tm
    return pl.pallas_call(
        functools.partial(_norm_mod2_kernel, nh=nh),
        grid=(N_TOK // tm,),
        in_specs=[pl.BlockSpec((tm, D_MODEL), lambda i: (jnp.minimum(i, nh - 1), 0)),
                  pl.BlockSpec((tm, D_MODEL), lambda i: (jnp.maximum(i - nh, 0), 0)),
                  pl.BlockSpec((1, D_MODEL), lambda i: (0, 0)),
                  _mod_spec(0, tm), _mod_spec(1, tm)],
        out_specs=pl.BlockSpec((tm, D_MODEL), lambda i: (i, 0)),
        out_shape=jax.ShapeDtypeStruct((N_TOK, D_MODEL), BF16),
        compiler_params=_params(1),
        name="norm1_mod",
    )(xp, xs, g, mod3, mod3)


def _norm2(x1, g, mod3):
    tm = 256
    return pl.pallas_call(
        _norm_mod1_kernel,
        grid=(N_TOK // tm,),
        in_specs=[pl.BlockSpec((tm, D_MODEL), lambda i: (i, 0)),
                  pl.BlockSpec((1, D_MODEL), lambda i: (0, 0)),
                  _mod_spec(3, tm), _mod_spec(4, tm)],
        out_specs=pl.BlockSpec((tm, D_MODEL), lambda i: (i, 0)),
        out_shape=jax.ShapeDtypeStruct((N_TOK, D_MODEL), BF16),
        compiler_params=_params(1),
        name="norm2_mod",
    )(x1, g, mod3, mod3)


CAST_ROWS = 512


def _cast_weight(w_ref, wb_ref):
    def body(c, carry):
        r = pl.multiple_of(c * CAST_ROWS, CAST_ROWS)
        wb_ref[pl.ds(r, CAST_ROWS), :] = w_ref[pl.ds(r, CAST_ROWS), :].astype(BF16)
        return carry
    lax.fori_loop(0, w_ref.shape[0] // CAST_ROWS, body, 0)


def _ws_kernel(x_ref, w_ref, o_ref, wb_ref):
    @pl.when(pl.program_id(1) == 0)
    def _():
        _cast_weight(w_ref, wb_ref)

    o_ref[...] = jnp.dot(x_ref[...], wb_ref[...],
                         preferred_element_type=F32).astype(o_ref.dtype)


def _ws_matmul(x, w, *, row_blk0, n_row_blks, col_map, n_col_blks, tm, tn, out_dtype, name):
    k = x.shape[1]
    return pl.pallas_call(
        _ws_kernel,
        grid=(n_col_blks, n_row_blks),
        in_specs=[pl.BlockSpec((tm, k), lambda j, i: (row_blk0 + i, 0)),
                  pl.BlockSpec((k, tn), lambda j, i: (0, col_map(j)))],
        out_specs=pl.BlockSpec((tm, tn), lambda j, i: (i, j)),
        out_shape=jax.ShapeDtypeStruct((n_row_blks * tm, n_col_blks * tn), out_dtype),
        scratch_shapes=[pltpu.VMEM((k, tn), BF16)],
        compiler_params=_params(2),
        name=name,
    )(x, w)


def _ctx_attn_kernel(q_ref, k_ref, v_ref, o_ref):
    for h in range(N_HEADS):
        hs = slice(h * HEAD_DIM, (h + 1) * HEAD_DIM)
        q = q_ref[:, hs]
        k = k_ref[:, hs].astype(BF16)
        v = v_ref[:, hs].astype(BF16)
        s = lax.dot_general(q, k, (((1,), (1,)), ((), ())), preferred_element_type=F32) * SCALE
        m = jnp.max(s, axis=-1, keepdims=True)
        p = jnp.exp(s - m)
        l = jnp.sum(p, axis=-1, keepdims=True)
        o = jnp.dot(p.astype(BF16), v, preferred_element_type=F32) / l
        o_ref[:, hs] = o.astype(o_ref.dtype)


def _ctx_attention(qu, newk, newv):
    return pl.pallas_call(
        _ctx_attn_kernel,
        grid=(BATCH,),
        in_specs=[pl.BlockSpec((SEQ, D_ATT), lambda b: (b, 0)),
                  pl.BlockSpec((SEQ, D_ATT), lambda b: (b, 0)),
                  pl.BlockSpec((SEQ, D_ATT), lambda b: (b, 0))],
        out_specs=pl.BlockSpec((SEQ, D_ATT), lambda b: (b, 0)),
        out_shape=jax.ShapeDtypeStruct((NP_TOK, D_ATT), BF16),
        compiler_params=_params(1),
        name="ctx_attention",
    )(qu, newk, newv)


def _rpb_bias_kernel(rpb_ref, o_ref, t_ref):
    h = pl.program_id(0)
    qc = lax.broadcasted_iota(jnp.int32, (GRID_W, GRID_W), 0)
    kc = lax.broadcasted_iota(jnp.int32, (GRID_W, GRID_W), 1)
    dc = jnp.clip(kc - qc + (WIN_W - 1), 0, 2 * WIN_W - 2)
    cstart = jnp.clip(qc - WIN_W // 2, 0, GRID_W - WIN_W)
    mask = (kc >= cstart) & (kc < cstart + WIN_W)
    n_dc = 2 * WIN_W - 1
    n_dr = 2 * WIN_H - 1
    for dr in range(n_dr):
        t = jnp.zeros((GRID_W, GRID_W), F32)
        for d in range(n_dc):
            t = jnp.where(dc == d, rpb_ref[h * (n_dr * n_dc) + dr * n_dc + d], t)
        t_ref[dr] = jnp.where(mask, t, NEG_INF)
    for cls in range(N_RCLASS):
        r = cls if cls <= 4 else cls + (GRID_ROWS - N_RCLASS)
        rstart = min(max(r - WIN_H // 2, 0), GRID_ROWS - WIN_H)
        for j in range(WIN_H):
            dr = rstart + j - r + WIN_H - 1
            o_ref[cls, 0, :, j * GRID_W:(j + 1) * GRID_W] = t_ref[dr]


def _rpb_bias(rpb_flat):
    return pl.pallas_call(
        _rpb_bias_kernel,
        grid=(N_HEADS,),
        in_specs=[pl.BlockSpec(memory_space=pltpu.SMEM)],
        out_specs=pl.BlockSpec((N_RCLASS, 1, GRID_W, N_LOC), lambda h: (0, h, 0, 0)),
        out_shape=jax.ShapeDtypeStruct((N_RCLASS, N_HEADS, GRID_W, N_LOC), F32),
        scratch_shapes=[pltpu.VMEM((2 * WIN_H - 1, GRID_W, GRID_W), F32)],
        compiler_params=_params(1),
        name="rpb_bias",
    )(rpb_flat)


NA_HEADS = 4
NA_COLS = NA_HEADS * HEAD_DIM


def _nbr_attn_kernel(q_ref, k_ref, v_ref, kc_ref, vc_ref, bias_ref, o_ref, kcb_ref, vcb_ref):
    kcb_ref[...] = kc_ref[...].astype(BF16)
    vcb_ref[...] = vc_ref[...].astype(BF16)

    def row_body(r, carry):
        rstart = jnp.clip(r - WIN_H // 2, 0, GRID_ROWS - WIN_H)
        cls = jnp.where(r < WIN_H // 2, r,
                        jnp.where(r > GRID_ROWS - WIN_H // 2, r - (GRID_ROWS - N_RCLASS), WIN_H // 2))
        q0 = pl.multiple_of(r * GRID_W, GRID_W)
        k0 = pl.multiple_of(rstart * GRID_W, GRID_W)
        for h in range(NA_HEADS):
            hs = slice(h * HEAD_DIM, (h + 1) * HEAD_DIM)
            q = q_ref[pl.ds(q0, GRID_W), hs]
            kw = k_ref[pl.ds(k0, N_LOC), hs]
            vw = v_ref[pl.ds(k0, N_LOC), hs]
            dn = (((1,), (1,)), ((), ()))
            s_loc = lax.dot_general(q, kw, dn, preferred_element_type=F32) * SCALE + bias_ref[cls, h]
            s_ctx = lax.dot_general(q, kcb_ref[:, hs], dn, preferred_element_type=F32) * SCALE
            m = jnp.maximum(jnp.max(s_loc, axis=-1, keepdims=True),
                            jnp.max(s_ctx, axis=-1, keepdims=True))
            p_loc = jnp.exp(s_loc - m)
            p_ctx = jnp.exp(s_ctx - m)
            l = jnp.sum(p_loc, axis=-1, keepdims=True) + jnp.sum(p_ctx, axis=-1, keepdims=True)
            o = (jnp.dot(p_loc.astype(BF16), vw, preferred_element_type=F32)
                 + jnp.dot(p_ctx.astype(BF16), vcb_ref[:, hs], preferred_element_type=F32)) / l
            o_ref[pl.ds(q0, GRID_W), hs] = o.astype(o_ref.dtype)
        return carry

    lax.fori_loop(0, GRID_ROWS, row_body, 0)


def _nbr_attention(qu, kv_s, ck, cv, bias):
    n_hg = N_HEADS // NA_HEADS
    row_blk0 = NP_TOK // DEC_SEQ
    return pl.pallas_call(
        _nbr_attn_kernel,
        grid=(DEC_BATCH, n_hg),
        in_specs=[pl.BlockSpec((DEC_SEQ, NA_COLS), lambda b, g: (row_blk0 + b, g)),
                  pl.BlockSpec((DEC_SEQ, NA_COLS), lambda b, g: (b, g)),
                  pl.BlockSpec((DEC_SEQ, NA_COLS), lambda b, g: (b, n_hg + g)),
                  pl.BlockSpec((PAST_LEN, NA_COLS), lambda b, g: (b, g)),
                  pl.BlockSpec((PAST_LEN, NA_COLS), lambda b, g: (b, g)),
                  pl.BlockSpec((N_RCLASS, NA_HEADS, GRID_W, N_LOC), lambda b, g: (0, g, 0, 0))],
        out_specs=pl.BlockSpec((DEC_SEQ, NA_COLS), lambda b, g: (b, g)),
        out_shape=jax.ShapeDtypeStruct((NS_TOK, D_ATT), BF16),
        scratch_shapes=[pltpu.VMEM((PAST_LEN, NA_COLS), BF16),
                        pltpu.VMEM((PAST_LEN, NA_COLS), BF16)],
        compiler_params=_params(2),
        name="nbr_attention",
    )(qu, kv_s, kv_s, ck, cv, bias)


def _dft_tables(n):
    idx = np.arange(n, dtype=np.int64)
    ang = (2.0 * np.pi / n) * ((idx[:, None] * idx[None, :]) % n).astype(np.float64)
    return ((np.cos(ang) / np.sqrt(n)).astype(np.float32),
            (-np.sin(ang) / np.sqrt(n)).astype(np.float32))


def _dft_chan_kernel(u_ref, w_ref, o_ref):
    for g in range(N_FGROUPS):
        u = u_ref[:, g * FGROUP_DIM:(g + 1) * FGROUP_DIM]
        o_ref[:, g * 2 * FGROUP_DIM:(g + 1) * 2 * FGROUP_DIM] = jnp.dot(
            u, w_ref[...], preferred_element_type=F32).astype(o_ref.dtype)


def _dft_chan(qu, w1):
    tm = 1024
    return pl.pallas_call(
        _dft_chan_kernel,
        grid=(N_TOK // tm,),
        in_specs=[pl.BlockSpec((tm, D_FOURIER), lambda i: (i, 1)),
                  pl.BlockSpec((FGROUP_DIM, 2 * FGROUP_DIM), lambda i: (0, 0))],
        out_specs=pl.BlockSpec((tm, 2 * D_FOURIER), lambda i: (i, 0)),
        out_shape=jax.ShapeDtypeStruct((N_TOK, 2 * D_FOURIER), BF16),
        compiler_params=_params(1),
        name="dft_channels",
    )(qu, w1)


def _dft_pos_kernel(ct_ref, st_ref, ab_ref, o_ref):
    for g in range(ab_ref.shape[1] // (2 * FGROUP_DIM)):
        a = ab_ref[:, g * 2 * FGROUP_DIM:g * 2 * FGROUP_DIM + FGROUP_DIM]
        b = ab_ref[:, g * 2 * FGROUP_DIM + FGROUP_DIM:(g + 1) * 2 * FGROUP_DIM]
        o = (jnp.dot(ct_ref[...], a, preferred_element_type=F32)
             + jnp.dot(st_ref[...], b, preferred_element_type=F32))
        o_ref[:, g * FGROUP_DIM:(g + 1) * FGROUP_DIM] = o.astype(o_ref.dtype)


def _dft_pos_prompt(ab, ct, st):
    return pl.pallas_call(
        _dft_pos_kernel,
        grid=(BATCH,),
        in_specs=[pl.BlockSpec((SEQ, SEQ), lambda b: (0, 0)),
                  pl.BlockSpec((SEQ, SEQ), lambda b: (0, 0)),
                  pl.BlockSpec((SEQ, 2 * D_FOURIER), lambda b: (b, 0))],
        out_specs=pl.BlockSpec((SEQ, D_FOURIER), lambda b: (b, 0)),
        out_shape=jax.ShapeDtypeStruct((NP_TOK, D_FOURIER), BF16),
        compiler_params=_params(1),
        name="dft_pos_prompt",
    )(ct, st, ab)


def _dft_pos_sample(ab, ct, st):
    tr = 512
    row_blk0 = NP_TOK // DEC_SEQ
    return pl.pallas_call(
        _dft_pos_kernel,
        grid=(DEC_BATCH, N_FGROUPS, DEC_SEQ // tr),
        in_specs=[pl.BlockSpec((tr, DEC_SEQ), lambda b, g, t: (t, 0)),
                  pl.BlockSpec((tr, DEC_SEQ), lambda b, g, t: (t, 0)),
                  pl.BlockSpec((DEC_SEQ, 2 * FGROUP_DIM), lambda b, g, t: (row_blk0 + b, g))],
        out_specs=pl.BlockSpec((tr, FGROUP_DIM), lambda b, g, t: (b * (DEC_SEQ // tr) + t, g)),
        out_shape=jax.ShapeDtypeStruct((NS_TOK, D_FOURIER), BF16),
        compiler_params=_params(3),
        name="dft_pos_sample",
    )(ct, st, ab)


def _wout_kernel(ap_ref, as_ref, fp_ref, fs_ref, w_ref, xp_ref, xs_ref, g_ref, o_ref, wb_ref, *, nh):
    i = pl.program_id(1)

    @pl.when(i == 0)
    def _():
        _cast_weight(w_ref, wb_ref)

    def run(a_ref, f_ref, x_ref):
        acc = (jnp.dot(a_ref[...], wb_ref[0:D_ATT, :], preferred_element_type=F32)
               + jnp.dot(f_ref[...], wb_ref[D_ATT:D_ATT + D_FOURIER, :], preferred_element_type=F32))
        o_ref[...] = x_ref[...] + g_ref[0] * acc

    @pl.when(i < nh)
    def _():
        run(ap_ref, fp_ref, xp_ref)

    @pl.when(i >= nh)
    def _():
        run(as_ref, fs_ref, xs_ref)


def _wout(att_p, att_s, fou_p, fou_s, w_out, xp, xs, mod3):
    tm, tn = 512, 512
    nh = NP_TOK // tm
    lo = lambda j, i: (jnp.minimum(i, nh - 1), 0)
    hi = lambda j, i: (jnp.maximum(i - nh, 0), 0)
    return pl.pallas_call(
        functools.partial(_wout_kernel, nh=nh),
        grid=(D_MODEL // tn, N_TOK // tm),
        in_specs=[pl.BlockSpec((tm, D_ATT), lo), pl.BlockSpec((tm, D_ATT), hi),
                  pl.BlockSpec((tm, D_FOURIER), lo), pl.BlockSpec((tm, D_FOURIER), hi),
                  pl.BlockSpec((D_MODEL, tn), lambda j, i: (0, j)),
                  pl.BlockSpec((tm, tn), lambda j, i: (jnp.minimum(i, nh - 1), j)),
                  pl.BlockSpec((tm, tn), lambda j, i: (jnp.maximum(i - nh, 0), j)),
                  pl.BlockSpec((1, 1, tn), lambda j, i: (_mod_row(i, tm) * 6 + 2, 0, j))],
        out_specs=pl.BlockSpec((tm, tn), lambda j, i: (i, j)),
        out_shape=jax.ShapeDtypeStruct((N_TOK, D_MODEL), F32),
        scratch_shapes=[pltpu.VMEM((D_MODEL, tn), BF16)],
        compiler_params=_params(2),
        name="w_out_residual",
    )(att_p, att_s, fou_p, fou_s, w_out, xp, xs, mod3)


FF_TN = 256


def _gate_up_kernel(h_ref, wg_ref, wu_ref, o_ref, wgb_ref, wub_ref, *, n_real):
    j = pl.program_id(0)

    @pl.when((pl.program_id(1) == 0) & (j < n_real))
    def _():
        _cast_weight(wg_ref, wgb_ref)
        _cast_weight(wu_ref, wub_ref)

    @pl.when(j < n_real)
    def _():
        h = h_ref[...]
        g = jnp.dot(h, wgb_ref[...], preferred_element_type=F32)
        u = jnp.dot(h, wub_ref[...], preferred_element_type=F32)
        o_ref[...] = (g * jax.nn.sigmoid(g) * u).astype(o_ref.dtype)

    @pl.when(j >= n_real)
    def _():
        o_ref[...] = jnp.zeros_like(o_ref)


def _gate_up(h2, w_gate, w_up):
    tm, tn = 1024, FF_TN
    n_real = D_FF // tn
    wmap = lambda j, i: (0, jnp.minimum(j, n_real - 1))
    return pl.pallas_call(
        functools.partial(_gate_up_kernel, n_real=n_real),
        grid=(D_FF_PAD // tn, N_TOK // tm),
        in_specs=[pl.BlockSpec((tm, D_MODEL), lambda j, i: (i, 0)),
                  pl.BlockSpec((D_MODEL, tn), wmap),
                  pl.BlockSpec((D_MODEL, tn), wmap)],
        out_specs=pl.BlockSpec((tm, tn), lambda j, i: (i, j)),
        out_shape=jax.ShapeDtypeStruct((N_TOK, D_FF_PAD), BF16),
        scratch_shapes=[pltpu.VMEM((D_MODEL, tn), BF16), pltpu.VMEM((D_MODEL, tn), BF16)],
        compiler_params=_params(2),
        name="ffn_gate_up",
    )(h2, w_gate, w_up)


def _cast_pad_kernel(w_ref, o_ref, *, n_real):
    r = pl.program_id(0)

    @pl.when(r < n_real)
    def _():
        o_ref[...] = w_ref[...].astype(o_ref.dtype)

    @pl.when(r >= n_real)
    def _():
        o_ref[...] = jnp.zeros_like(o_ref)


def _cast_pad_wdown(w_down):
    tr = 256
    n_real = D_FF // tr
    return pl.pallas_call(
        functools.partial(_cast_pad_kernel, n_real=n_real),
        grid=(D_FF_PAD // tr,),
        in_specs=[pl.BlockSpec((tr, D_MODEL), lambda r: (jnp.minimum(r, n_real - 1), 0))],
        out_specs=pl.BlockSpec((tr, D_MODEL), lambda r: (r, 0)),
        out_shape=jax.ShapeDtypeStruct((D_FF_PAD, D_MODEL), BF16),
        compiler_params=_params(1),
        name="w_down_cast",
    )(w_down)


def _down_kernel(a_ref, w_ref, x_ref, g_ref, fg_ref, o_ref, *, nk):
    k = pl.program_id(1)

    @pl.when(k == 0)
    def _():
        o_ref[...] = jnp.zeros_like(o_ref)

    for n in range(D_MODEL // DOWN_TN):
        ns = slice(n * DOWN_TN, (n + 1) * DOWN_TN)
        o_ref[:, ns] += jnp.dot(a_ref[...], w_ref[:, ns], preferred_element_type=F32)

    @pl.when(k == nk - 1)
    def _():
        def chunk(rows):
            x2 = x_ref[rows, :] + g_ref[0] * o_ref[rows, :]
            ms = jnp.mean(x2 * x2, axis=-1, keepdims=True)
            o_ref[rows, :] = x2 * lax.rsqrt(ms + EPS) * fg_ref[...]
        _row_chunks(o_ref.shape[0], chunk)


DOWN_TN = 512


def _down(a, wd, x1, mod3, final_g):
    tm, tk = 512, 512
    nk = D_FF_PAD // tk
    return pl.pallas_call(
        functools.partial(_down_kernel, nk=nk),
        grid=(N_TOK // tm, nk),
        in_specs=[pl.BlockSpec((tm, tk), lambda i, k: (i, k)),
                  pl.BlockSpec((tk, D_MODEL), lambda i, k: (k, 0)),
                  pl.BlockSpec((tm, D_MODEL), lambda i, k: (i, 0)),
                  pl.BlockSpec((1, 1, D_MODEL), lambda i, k: (_mod_row(i, tm) * 6 + 5, 0, 0)),
                  pl.BlockSpec((1, D_MODEL), lambda i, k: (0, 0))],
        out_specs=pl.BlockSpec((tm, D_MODEL), lambda i, k: (i, 0)),
        out_shape=jax.ShapeDtypeStruct((N_TOK, D_MODEL), F32),
        compiler_params=_params(2),
        name="ffn_down_final",
    )(a, wd, x1, mod3, final_g)


def kernel(x_prompt, x_sample, cache_k, cache_v, c, c_ctx, w_ada, b_ada, norm1_g, w_in, rpb,
           w_out, norm2_g, w_gate, w_up, w_down, final_g):
    xp = x_prompt.reshape(NP_TOK, D_MODEL)
    xs = x_sample.reshape(NS_TOK, D_MODEL)

    cvec = jnp.concatenate([c_ctx[None, :], c, jnp.zeros((N_MOD - 1 - DEC_BATCH, D_MODEL), F32)], axis=0)
    mod = _ada(cvec, w_ada[0], b_ada[0][None, :])
    mod3 = mod.reshape(N_MOD * 6, 1, D_MODEL)

    h = _norm1(xp, xs, norm1_g[0][None, :], mod3)

    w_in0 = w_in[0]
    tm, tn = 1024, 512
    n_att_blks = D_ATT // tn
    qu = _ws_matmul(h, w_in0, row_blk0=0, n_row_blks=N_TOK // tm,
                    col_map=lambda j: jnp.where(j < n_att_blks, j, j + 2 * n_att_blks),
                    n_col_blks=2 * n_att_blks, tm=tm, tn=tn, out_dtype=BF16, name="w_in_q_u")
    newk = _ws_matmul(h, w_in0, row_blk0=0, n_row_blks=NP_TOK // tm,
                      col_map=lambda j: j + n_att_blks, n_col_blks=n_att_blks,
                      tm=tm, tn=tn, out_dtype=F32, name="w_in_k_prompt")
    newv = _ws_matmul(h, w_in0, row_blk0=0, n_row_blks=NP_TOK // tm,
                      col_map=lambda j: j + 2 * n_att_blks, n_col_blks=n_att_blks,
                      tm=tm, tn=tn, out_dtype=F32, name="w_in_v_prompt")
    kv_s = _ws_matmul(h, w_in0, row_blk0=NP_TOK // tm, n_row_blks=NS_TOK // tm,
                      col_map=lambda j: j + n_att_blks, n_col_blks=2 * n_att_blks,
                      tm=tm, tn=tn, out_dtype=BF16, name="w_in_kv_sample")

    att_p = _ctx_attention(qu, newk, newv)
    bias = _rpb_bias(rpb[0].reshape(-1))
    ck = cache_k[:, 0].reshape(DEC_BATCH * PAST_LEN, D_ATT)
    cv = cache_v[:, 0].reshape(DEC_BATCH * PAST_LEN, D_ATT)
    att_s = _nbr_attention(qu, kv_s, ck, cv, bias)

    cc, sc = _dft_tables(FGROUP_DIM)
    w1 = jnp.asarray(np.concatenate([cc, -sc], axis=1)).astype(BF16)
    ab = _dft_chan(qu, w1)
    ctp, stp = _dft_tables(SEQ)
    fou_p = _dft_pos_prompt(ab, jnp.asarray(ctp).astype(BF16), jnp.asarray(stp).astype(BF16))
    cts, sts = _dft_tables(DEC_SEQ)
    fou_s = _dft_pos_sample(ab, jnp.asarray(cts).astype(BF16), jnp.asarray(sts).astype(BF16))

    x1 = _wout(att_p, att_s, fou_p, fou_s, w_out[0], xp, xs, mod3)

    h2 = _norm2(x1, norm2_g[0][None, :], mod3)
    a = _gate_up(h2, w_gate[0], w_up[0])
    wd = _cast_pad_wdown(w_down[0])
    y = _down(a, wd, x1, mod3, final_g[None, :])

    y_prompt = y[:NP_TOK].reshape(BATCH, SEQ, D_MODEL)
    y_sample = y[NP_TOK:].reshape(DEC_BATCH, DEC_SEQ, D_MODEL)
    new_cache_k = newk.reshape(BATCH, 1, SEQ, N_HEADS, HEAD_DIM)
    new_cache_v = newv.reshape(BATCH, 1, SEQ, N_HEADS, HEAD_DIM)
    return (y_prompt, y_sample, new_cache_k, new_cache_v)
```

```python
import functools

import numpy as np
import jax
import jax.numpy as jnp
from jax import lax
from jax.experimental import pallas as pl
from jax.experimental.pallas import tpu as pltpu

F32 = jnp.float32
BF16 = jnp.bfloat16

D_MODEL = 4096
BATCH = 32
SEQ = 256
DEC_BATCH = 4
DEC_SEQ = 2048
PAST_LEN = 256
GRID_W = 64
GRID_ROWS = DEC_SEQ // GRID_W
D_ATT = 2048
D_FOURIER = 2048
HEAD_DIM = 128
N_HEADS = 16
N_FGROUPS = 4
FGROUP_DIM = 512
WIN_H = 8
WIN_W = 16
D_FF = 11008
D_FF_PAD = 11264
EPS = 1e-6
NEG_INF = -1e30
SCALE = HEAD_DIM ** -0.5

NP_TOK = BATCH * SEQ
NS_TOK = DEC_BATCH * DEC_SEQ
N_TOK = NP_TOK + NS_TOK
N_MOD = 8

VMEM_LIMIT = 56 * 1024 * 1024


def _params(n_axes, vmem=VMEM_LIMIT):
    return pltpu.CompilerParams(dimension_semantics=("arbitrary",) * n_axes,
                                vmem_limit_bytes=vmem)


def _mod_row(i, tm):
    nh = NP_TOK // tm
    return jnp.where(i < nh, 0, 1 + (i - nh) // (DEC_SEQ // tm))


def _ada_kernel(c_ref, w_ref, b_ref, o_ref):
    c = c_ref[...]
    s = (c * jax.nn.sigmoid(c)).astype(BF16)
    o_ref[...] = jnp.dot(s, w_ref[...].astype(BF16), preferred_element_type=F32) + b_ref[...]


def _ada(cvec, w_ada, b_ada):
    tn = 512
    n = w_ada.shape[1]
    return pl.pallas_call(
        _ada_kernel,
        grid=(n // tn,),
        in_specs=[pl.BlockSpec((N_MOD, D_MODEL), lambda j: (0, 0)),
                  pl.BlockSpec((D_MODEL, tn), lambda j: (0, j)),
                  pl.BlockSpec((1, tn), lambda j: (0, j))],
        out_specs=pl.BlockSpec((N_MOD, tn), lambda j: (0, j)),
        out_shape=jax.ShapeDtypeStruct((N_MOD, n), F32),
        compiler_params=_params(1),
        name="ada_mod",
    )(cvec, w_ada, b_ada)


ROW_CHUNK = 8


def _row_chunks(n_rows, fn):
    def body(c, carry):
        fn(pl.ds(pl.multiple_of(c * ROW_CHUNK, ROW_CHUNK), ROW_CHUNK))
        return carry
    lax.fori_loop(0, n_rows // ROW_CHUNK, body, 0)


def _norm_mod(x_ref, g_ref, sh_ref, sc_ref, o_ref):
    def pair(c, carry):
        r0 = pl.multiple_of(c * 2 * ROW_CHUNK, 2 * ROW_CHUNK)
        halves = []
        for s in range(2):
            x = x_ref[pl.ds(r0 + s * ROW_CHUNK, ROW_CHUNK), :]
            ms = jnp.mean(x * x, axis=-1, keepdims=True)
            y = x * lax.rsqrt(ms + EPS) * g_ref[...]
            halves.append(y * (1.0 + sc_ref[0]) + sh_ref[0])
        o_ref[pl.ds(r0, 2 * ROW_CHUNK), :] = jnp.concatenate(halves, axis=0).astype(o_ref.dtype)
        return carry
    lax.fori_loop(0, o_ref.shape[0] // (2 * ROW_CHUNK), pair, 0)


def _norm_mod2_kernel(xp_ref, xs_ref, g_ref, sh_ref, sc_ref, o_ref, *, nh):
    i = pl.program_id(0)

    @pl.when(i < nh)
    def _():
        _norm_mod(xp_ref, g_ref, sh_ref, sc_ref, o_ref)

    @pl.when(i >= nh)
    def _():
        _norm_mod(xs_ref, g_ref, sh_ref, sc_ref, o_ref)


def _norm_mod1_kernel(x_ref, g_ref, sh_ref, sc_ref, o_ref):
    _norm_mod(x_ref, g_ref, sh_ref, sc_ref, o_ref)


def _mod_spec(part, tm):
    return pl.BlockSpec((1, 1, D_MODEL), lambda i: (_mod_row(i, tm) * 6 + part, 0, 0))


def _norm1(xp, xs, g, mod3):
    tm = 256
    nh = NP_TOK // tm
    return pl.pallas_call(
        functools.partial(_norm_mod2_kernel, nh=nh),
        grid=(N_TOK // tm,),
        in_specs=[pl.BlockSpec((tm, D_MODEL), lambda i: (jnp.minimum(i, nh - 1), 0)),
                  pl.BlockSpec((tm, D_MODEL), lambda i: (jnp.maximum(i - nh, 0), 0)),
                  pl.BlockSpec((1, D_MODEL), lambda i: (0, 0)),
                  _mod_spec(0, tm), _mod_spec(1, tm)],
        out_specs=pl.BlockSpec((tm, D_MODEL), lambda i: (i, 0)),
        out_shape=jax.ShapeDtypeStruct((N_TOK, D_MODEL), BF16),
        compiler_params=_params(1),
        name="norm1_mod",
    )(xp, xs, g, mod3, mod3)


def _norm2(x1, g, mod3):
    tm = 256
    return pl.pallas_call(
        _norm_mod1_kernel,
        grid=(N_TOK // tm,),
        in_specs=[pl.BlockSpec((tm, D_MODEL), lambda i: (i, 0)),
                  pl.BlockSpec((1, D_MODEL), lambda i: (0, 0)),
                  _mod_spec(3, tm), _mod_spec(4, tm)],
        out_specs=pl.BlockSpec((tm, D_MODEL), lambda i: (i, 0)),
        out_shape=jax.ShapeDtypeStruct((N_TOK, D_MODEL), BF16),
        compiler_params=_params(1),
        name="norm2_mod",
    )(x1, g, mod3, mod3)


CAST_ROWS = 512


def _cast_weight(w_ref, wb_ref):
    def body(c, carry):
        r = pl.multiple_of(c * CAST_ROWS, CAST_ROWS)
        wb_ref[pl.ds(r, CAST_ROWS), :] = w_ref[pl.ds(r, CAST_ROWS), :].astype(BF16)
        return carry
    lax.fori_loop(0, w_ref.shape[0] // CAST_ROWS, body, 0)


def _ws_kernel(x_ref, w_ref, o_ref, wb_ref):
    @pl.when(pl.program_id(1) == 0)
    def _():
        _cast_weight(w_ref, wb_ref)

    o_ref[...] = jnp.dot(x_ref[...], wb_ref[...],
                         preferred_element_type=F32).astype(o_ref.dtype)


def _ws_matmul(x, w, *, row_blk0, n_row_blks, col_map, n_col_blks, tm, tn, out_dtype, name):
    k = x.shape[1]
    return pl.pallas_call(
        _ws_kernel,
        grid=(n_col_blks, n_row_blks),
        in_specs=[pl.BlockSpec((tm, k), lambda j, i: (row_blk0 + i, 0)),
                  pl.BlockSpec((k, tn), lambda j, i: (0, col_map(j)))],
        out_specs=pl.BlockSpec((tm, tn), lambda j, i: (i, j)),
        out_shape=jax.ShapeDtypeStruct((n_row_blks * tm, n_col_blks * tn), out_dtype),
        scratch_shapes=[pltpu.VMEM((k, tn), BF16)],
        compiler_params=_params(2),
        name=name,
    )(x, w)


def _ctx_attn_kernel(q_ref, k_ref, v_ref, o_ref):
    for h in range(N_HEADS):
        hs = slice(h * HEAD_DIM, (h + 1) * HEAD_DIM)
        q = q_ref[:, hs]
        k = k_ref[:, hs].astype(BF16)
        v = v_ref[:, hs].astype(BF16)
        s = lax.dot_general(q, k, (((1,), (1,)), ((), ())), preferred_element_type=F32) * SCALE
        m = jnp.max(s, axis=-1, keepdims=True)
        p = jnp.exp(s - m)
        l = jnp.sum(p, axis=-1, keepdims=True)
        o = jnp.dot(p.astype(BF16), v, preferred_element_type=F32) / l
        o_ref[:, hs] = o.astype(o_ref.dtype)


def _ctx_attention(qu, newk, newv):
    return pl.pallas_call(
        _ctx_attn_kernel,
        grid=(BATCH,),
        in_specs=[pl.BlockSpec((SEQ, D_ATT), lambda b: (b, 0)),
                  pl.BlockSpec((SEQ, D_ATT), lambda b: (b, 0)),
                  pl.BlockSpec((SEQ, D_ATT), lambda b: (b, 0))],
        out_specs=pl.BlockSpec((SEQ, D_ATT), lambda b: (b, 0)),
        out_shape=jax.ShapeDtypeStruct((NP_TOK, D_ATT), BF16),
        compiler_params=_params(1),
        name="ctx_attention",
    )(qu, newk, newv)


NA_QROWS = 4
NA_KROWS = NA_QROWS + WIN_H
NA_Q = NA_QROWS * GRID_W
NA_K = NA_KROWS * GRID_W
NA_GROUPS = GRID_ROWS // NA_QROWS


def _nbr_window_start(g):
    lo, hi = 0, GRID_ROWS - NA_KROWS
    if isinstance(g, int):
        return min(max(NA_QROWS * g - WIN_H // 2, lo), hi)
    return jnp.clip(NA_QROWS * g - WIN_H // 2, lo, hi)


def _nbr_classes():
    patterns, cls_of_g = [], []
    for g in range(NA_GROUPS):
        start = _nbr_window_start(g)
        pat = []
        for i in range(NA_QROWS):
            r = NA_QROWS * g + i
            rstart = min(max(r - WIN_H // 2, 0), GRID_ROWS - WIN_H)
            pat.append(tuple((start + j - r + WIN_H - 1) if rstart <= start + j < rstart + WIN_H else None
                             for j in range(NA_KROWS)))
        pat = tuple(pat)
        if pat not in patterns:
            patterns.append(pat)
        cls_of_g.append(patterns.index(pat))
    return tuple(cls_of_g), tuple(patterns)


def _rpb_bias_kernel(rpb_ref, o_ref, t_ref, *, patterns):
    h = pl.program_id(0)
    qc = lax.broadcasted_iota(jnp.int32, (GRID_W, GRID_W), 0)
    kc = lax.broadcasted_iota(jnp.int32, (GRID_W, GRID_W), 1)
    dc = jnp.clip(kc - qc + (WIN_W - 1), 0, 2 * WIN_W - 2)
    cstart = jnp.clip(qc - WIN_W // 2, 0, GRID_W - WIN_W)
    mask = (kc >= cstart) & (kc < cstart + WIN_W)
    n_dc = 2 * WIN_W - 1
    n_dr = 2 * WIN_H - 1
    for dr in range(n_dr):
        t = jnp.zeros((GRID_W, GRID_W), F32)
        for d in range(n_dc):
            t = jnp.where(dc == d, rpb_ref[h * (n_dr * n_dc) + dr * n_dc + d], t)
        t_ref[dr] = jnp.where(mask, t, NEG_INF)
    outside = jnp.full((GRID_W, GRID_W), NEG_INF, F32)
    for c, pat in enumerate(patterns):
        for i in range(NA_QROWS):
            for j in range(NA_KROWS):
                dr = pat[i][j]
                o_ref[c, 0, i * GRID_W:(i + 1) * GRID_W, j * GRID_W:(j + 1) * GRID_W] = (
                    outside if dr is None else t_ref[dr])


def _rpb_bias(rpb_flat, patterns):
    n_cls = len(patterns)
    return pl.pallas_call(
        functools.partial(_rpb_bias_kernel, patterns=patterns),
        grid=(N_HEADS,),
        in_specs=[pl.BlockSpec(memory_space=pltpu.SMEM)],
        out_specs=pl.BlockSpec((n_cls, 1, NA_Q, NA_K), lambda h: (0, h, 0, 0)),
        out_shape=jax.ShapeDtypeStruct((n_cls, N_HEADS, NA_Q, NA_K), F32),
        scratch_shapes=[pltpu.VMEM((2 * WIN_H - 1, GRID_W, GRID_W), F32)],
        compiler_params=_params(1),
        name="rpb_bias",
    )(rpb_flat)


NA_HEADS = 4
NA_COLS = NA_HEADS * HEAD_DIM


def _nbr_attn_kernel(q_ref, k_ref, v_ref, kc_ref, vc_ref, bias_ref, o_ref, kcb_ref, vcb_ref, *,
                     cls_of_g):
    kcb_ref[...] = kc_ref[...].astype(BF16)
    vcb_ref[...] = vc_ref[...].astype(BF16)
    dn = (((1,), (1,)), ((), ()))

    def group_body(g, carry):
        cls = jnp.int32(cls_of_g[0])
        for gg in range(1, NA_GROUPS):
            if cls_of_g[gg] != cls_of_g[gg - 1]:
                cls = jnp.where(g >= gg, cls_of_g[gg], cls)
        q0 = pl.multiple_of(g * NA_Q, NA_Q)
        k0 = pl.multiple_of(_nbr_window_start(g) * GRID_W, GRID_W)
        for h in range(NA_HEADS):
            hs = slice(h * HEAD_DIM, (h + 1) * HEAD_DIM)
            q = q_ref[pl.ds(q0, NA_Q), hs]
            kw = k_ref[pl.ds(k0, NA_K), hs]
            vw = v_ref[pl.ds(k0, NA_K), hs]
            s_loc = lax.dot_general(q, kw, dn, preferred_element_type=F32) * SCALE + bias_ref[cls, h]
            s_ctx = lax.dot_general(q, kcb_ref[:, hs], dn, preferred_element_type=F32) * SCALE
            m = jnp.maximum(jnp.max(s_loc, axis=-1, keepdims=True),
                            jnp.max(s_ctx, axis=-1, keepdims=True))
            p_loc = jnp.exp(s_loc - m)
            p_ctx = jnp.exp(s_ctx - m)
            l = jnp.sum(p_loc, axis=-1, keepdims=True) + jnp.sum(p_ctx, axis=-1, keepdims=True)
            o = (jnp.dot(p_loc.astype(BF16), vw, preferred_element_type=F32)
                 + jnp.dot(p_ctx.astype(BF16), vcb_ref[:, hs], preferred_element_type=F32)) / l
            o_ref[pl.ds(q0, NA_Q), hs] = o.astype(o_ref.dtype)
        return carry

    lax.fori_loop(0, NA_GROUPS, group_body, 0)


def _nbr_attention(qu, kv_s, ck, cv, bias, cls_of_g):
    n_hg = N_HEADS // NA_HEADS
    n_cls = bias.shape[0]
    row_blk0 = NP_TOK // DEC_SEQ
    return pl.pallas_call(
        functools.partial(_nbr_attn_kernel, cls_of_g=cls_of_g),
        grid=(n_hg, DEC_BATCH),
        in_specs=[pl.BlockSpec((DEC_SEQ, NA_COLS), lambda g, b: (row_blk0 + b, g)),
                  pl.BlockSpec((DEC_SEQ, NA_COLS), lambda g, b: (b, g)),
                  pl.BlockSpec((DEC_SEQ, NA_COLS), lambda g, b: (b, n_hg + g)),
                  pl.BlockSpec((PAST_LEN, NA_COLS), lambda g, b: (b, g)),
                  pl.BlockSpec((PAST_LEN, NA_COLS), lambda g, b: (b, g)),
                  pl.BlockSpec((n_cls, NA_HEADS, NA_Q, NA_K), lambda g, b: (0, g, 0, 0))],
        out_specs=pl.BlockSpec((DEC_SEQ, NA_COLS), lambda g, b: (b, g)),
        out_shape=jax.ShapeDtypeStruct((NS_TOK, D_ATT), BF16),
        scratch_shapes=[pltpu.VMEM((PAST_LEN, NA_COLS), BF16),
                        pltpu.VMEM((PAST_LEN, NA_COLS), BF16)],
        compiler_params=_params(2),
        name="nbr_attention",
    )(qu, kv_s, kv_s, ck, cv, bias)


def _dft_tables(n):
    idx = np.arange(n, dtype=np.int64)
    ang = (2.0 * np.pi / n) * ((idx[:, None] * idx[None, :]) % n).astype(np.float64)
    return ((np.cos(ang) / np.sqrt(n)).astype(np.float32),
            (-np.sin(ang) / np.sqrt(n)).astype(np.float32))


def _dft_chan_kernel(u_ref, w_ref, o_ref):
    for g in range(N_FGROUPS):
        u = u_ref[:, g * FGROUP_DIM:(g + 1) * FGROUP_DIM]
        o_ref[:, g * 2 * FGROUP_DIM:(g + 1) * 2 * FGROUP_DIM] = jnp.dot(
            u, w_ref[...], preferred_element_type=F32).astype(o_ref.dtype)


def _dft_chan(qu, w1):
    tm = 1024
    return pl.pallas_call(
        _dft_chan_kernel,
        grid=(N_TOK // tm,),
        in_specs=[pl.BlockSpec((tm, D_FOURIER), lambda i: (i, 1)),
                  pl.BlockSpec((FGROUP_DIM, 2 * FGROUP_DIM), lambda i: (0, 0))],
        out_specs=pl.BlockSpec((tm, 2 * D_FOURIER), lambda i: (i, 0)),
        out_shape=jax.ShapeDtypeStruct((N_TOK, 2 * D_FOURIER), BF16),
        compiler_params=_params(1),
        name="dft_channels",
    )(qu, w1)


def _dft_pos_kernel(ct_ref, st_ref, ab_ref, o_ref):
    for g in range(ab_ref.shape[1] // (2 * FGROUP_DIM)):
        a = ab_ref[:, g * 2 * FGROUP_DIM:g * 2 * FGROUP_DIM + FGROUP_DIM]
        b = ab_ref[:, g * 2 * FGROUP_DIM + FGROUP_DIM:(g + 1) * 2 * FGROUP_DIM]
        o = (jnp.dot(ct_ref[...], a, preferred_element_type=F32)
             + jnp.dot(st_ref[...], b, preferred_element_type=F32))
        o_ref[:, g * FGROUP_DIM:(g + 1) * FGROUP_DIM] = o.astype(o_ref.dtype)


def _dft_pos_prompt(ab, ct, st):
    return pl.pallas_call(
        _dft_pos_kernel,
        grid=(BATCH,),
        in_specs=[pl.BlockSpec((SEQ, SEQ), lambda b: (0, 0)),
                  pl.BlockSpec((SEQ, SEQ), lambda b: (0, 0)),
                  pl.BlockSpec((SEQ, 2 * D_FOURIER), lambda b: (b, 0))],
        out_specs=pl.BlockSpec((SEQ, D_FOURIER), lambda b: (b, 0)),
        out_shape=jax.ShapeDtypeStruct((NP_TOK, D_FOURIER), BF16),
        compiler_params=_params(1),
        name="dft_pos_prompt",
    )(ct, st, ab)


def _dft_pos_sample(ab, ct, st):
    tr = 512
    row_blk0 = NP_TOK // DEC_SEQ
    return pl.pallas_call(
        _dft_pos_kernel,
        grid=(DEC_BATCH, N_FGROUPS, DEC_SEQ // tr),
        in_specs=[pl.BlockSpec((tr, DEC_SEQ), lambda b, g, t: (t, 0)),
                  pl.BlockSpec((tr, DEC_SEQ), lambda b, g, t: (t, 0)),
                  pl.BlockSpec((DEC_SEQ, 2 * FGROUP_DIM), lambda b, g, t: (row_blk0 + b, g))],
        out_specs=pl.BlockSpec((tr, FGROUP_DIM), lambda b, g, t: (b * (DEC_SEQ // tr) + t, g)),
        out_shape=jax.ShapeDtypeStruct((NS_TOK, D_FOURIER), BF16),
        compiler_params=_params(3),
        name="dft_pos_sample",
    )(ct, st, ab)


def _wout_kernel(ap_ref, as_ref, fp_ref, fs_ref, w_ref, xp_ref, xs_ref, g_ref, o_ref, wb_ref, *, nh):
    i = pl.program_id(1)

    @pl.when(i == 0)
    def _():
        _cast_weight(w_ref, wb_ref)

    def run(a_ref, f_ref, x_ref):
        acc = (jnp.dot(a_ref[...], wb_ref[0:D_ATT, :], preferred_element_type=F32)
               + jnp.dot(f_ref[...], wb_ref[D_ATT:D_ATT + D_FOURIER, :], preferred_element_type=F32))
        o_ref[...] = x_ref[...] + g_ref[0] * acc

    @pl.when(i < nh)
    def _():
        run(ap_ref, fp_ref, xp_ref)

    @pl.when(i >= nh)
    def _():
        run(as_ref, fs_ref, xs_ref)


def _wout(att_p, att_s, fou_p, fou_s, w_out, xp, xs, mod3):
    tm, tn = 512, 512
    nh = NP_TOK // tm
    lo = lambda j, i: (jnp.minimum(i, nh - 1), 0)
    hi = lambda j, i: (jnp.maximum(i - nh, 0), 0)
    return pl.pallas_call(
        functools.partial(_wout_kernel, nh=nh),
        grid=(D_MODEL // tn, N_TOK // tm),
        in_specs=[pl.BlockSpec((tm, D_ATT), lo), pl.BlockSpec((tm, D_ATT), hi),
                  pl.BlockSpec((tm, D_FOURIER), lo), pl.BlockSpec((tm, D_FOURIER), hi),
                  pl.BlockSpec((D_MODEL, tn), lambda j, i: (0, j)),
                  pl.BlockSpec((tm, tn), lambda j, i: (jnp.minimum(i, nh - 1), j)),
                  pl.BlockSpec((tm, tn), lambda j, i: (jnp.maximum(i - nh, 0), j)),
                  pl.BlockSpec((1, 1, tn), lambda j, i: (_mod_row(i, tm) * 6 + 2, 0, j))],
        out_specs=pl.BlockSpec((tm, tn), lambda j, i: (i, j)),
        out_shape=jax.ShapeDtypeStruct((N_TOK, D_MODEL), F32),
        scratch_shapes=[pltpu.VMEM((D_MODEL, tn), BF16)],
        compiler_params=_params(2),
        name="w_out_residual",
    )(att_p, att_s, fou_p, fou_s, w_out, xp, xs, mod3)


FF_TN = 256


def _gate_up_kernel(h_ref, wg_ref, wu_ref, o_ref, wgb_ref, wub_ref, *, n_real):
    j = pl.program_id(0)

    @pl.when((pl.program_id(1) == 0) & (j < n_real))
    def _():
        _cast_weight(wg_ref, wgb_ref)
        _cast_weight(wu_ref, wub_ref)

    @pl.when(j < n_real)
    def _():
        h = h_ref[...]
        g = jnp.dot(h, wgb_ref[...], preferred_element_type=F32)
        u = jnp.dot(h, wub_ref[...], preferred_element_type=F32)
        o_ref[...] = (g * jax.nn.sigmoid(g) * u).astype(o_ref.dtype)

    @pl.when(j >= n_real)
    def _():
        o_ref[...] = jnp.zeros_like(o_ref)


def _gate_up(h2, w_gate, w_up):
    tm, tn = 1024, FF_TN
    n_real = D_FF // tn
    wmap = lambda j, i: (0, jnp.minimum(j, n_real - 1))
    return pl.pallas_call(
        functools.partial(_gate_up_kernel, n_real=n_real),
        grid=(D_FF_PAD // tn, N_TOK // tm),
        in_specs=[pl.BlockSpec((tm, D_MODEL), lambda j, i: (i, 0)),
                  pl.BlockSpec((D_MODEL, tn), wmap),
                  pl.BlockSpec((D_MODEL, tn), wmap)],
        out_specs=pl.BlockSpec((tm, tn), lambda j, i: (i, j)),
        out_shape=jax.ShapeDtypeStruct((N_TOK, D_FF_PAD), BF16),
        scratch_shapes=[pltpu.VMEM((D_MODEL, tn), BF16), pltpu.VMEM((D_MODEL, tn), BF16)],
        compiler_params=_params(2),
        name="ffn_gate_up",
    )(h2, w_gate, w_up)


def _cast_pad_kernel(w_ref, o_ref, *, n_real):
    r = pl.program_id(0)

    @pl.when(r < n_real)
    def _():
        o_ref[...] = w_ref[...].astype(o_ref.dtype)

    @pl.when(r >= n_real)
    def _():
        o_ref[...] = jnp.zeros_like(o_ref)


def _cast_pad_wdown(w_down):
    tr = 256
    n_real = D_FF // tr
    return pl.pallas_call(
        functools.partial(_cast_pad_kernel, n_real=n_real),
        grid=(D_FF_PAD // tr,),
        in_specs=[pl.BlockSpec((tr, D_MODEL), lambda r: (jnp.minimum(r, n_real - 1), 0))],
        out_specs=pl.BlockSpec((tr, D_MODEL), lambda r: (r, 0)),
        out_shape=jax.ShapeDtypeStruct((D_FF_PAD, D_MODEL), BF16),
        compiler_params=_params(1),
        name="w_down_cast",
    )(w_down)


def _down_kernel(a_ref, w_ref, x_ref, g_ref, fg_ref, o_ref, *, nk):
    k = pl.program_id(1)

    @pl.when(k == 0)
    def _():
        o_ref[...] = jnp.zeros_like(o_ref)

    for n in range(D_MODEL // DOWN_TN):
        ns = slice(n * DOWN_TN, (n + 1) * DOWN_TN)
        o_ref[:, ns] += jnp.dot(a_ref[...], w_ref[:, ns], preferred_element_type=F32)

    @pl.when(k == nk - 1)
    def _():
        def chunk(rows):
            x2 = x_ref[rows, :] + g_ref[0] * o_ref[rows, :]
            ms = jnp.mean(x2 * x2, axis=-1, keepdims=True)
            o_ref[rows, :] = x2 * lax.rsqrt(ms + EPS) * fg_ref[...]
        _row_chunks(o_ref.shape[0], chunk)


DOWN_TN = 512


def _down(a, wd, x1, mod3, final_g, *, row0, n_rows, name):
    tm, tk = 512, 1024
    nk = D_FF_PAD // tk
    blk0 = row0 // tm
    return pl.pallas_call(
        functools.partial(_down_kernel, nk=nk),
        grid=(n_rows // tm, nk),
        in_specs=[pl.BlockSpec((tm, tk), lambda i, k: (blk0 + i, k)),
                  pl.BlockSpec((tk, D_MODEL), lambda i, k: (k, 0)),
                  pl.BlockSpec((tm, D_MODEL), lambda i, k: (blk0 + i, 0)),
                  pl.BlockSpec((1, 1, D_MODEL), lambda i, k: (_mod_row(blk0 + i, tm) * 6 + 5, 0, 0)),
                  pl.BlockSpec((1, D_MODEL), lambda i, k: (0, 0))],
        out_specs=pl.BlockSpec((tm, D_MODEL), lambda i, k: (i, 0)),
        out_shape=jax.ShapeDtypeStruct((n_rows, D_MODEL), F32),
        compiler_params=_params(2),
        name=name,
    )(a, wd, x1, mod3, final_g)


def kernel(x_prompt, x_sample, cache_k, cache_v, c, c_ctx, w_ada, b_ada, norm1_g, w_in, rpb,
           w_out, norm2_g, w_gate, w_up, w_down, final_g):
    xp = x_prompt.reshape(NP_TOK, D_MODEL)
    xs = x_sample.reshape(NS_TOK, D_MODEL)

    cvec = jnp.concatenate([c_ctx[None, :], c, jnp.zeros((N_MOD - 1 - DEC_BATCH, D_MODEL), F32)], axis=0)
    mod = _ada(cvec, w_ada[0], b_ada[0][None, :])
    mod3 = mod.reshape(N_MOD * 6, 1, D_MODEL)

    h = _norm1(xp, xs, norm1_g[0][None, :], mod3)

    w_in0 = w_in[0]
    tm, tn = 1024, 512
    n_att_blks = D_ATT // tn
    qu = _ws_matmul(h, w_in0, row_blk0=0, n_row_blks=N_TOK // tm,
                    col_map=lambda j: jnp.where(j < n_att_blks, j, j + 2 * n_att_blks),
                    n_col_blks=2 * n_att_blks, tm=tm, tn=tn, out_dtype=BF16, name="w_in_q_u")
    newk = _ws_matmul(h, w_in0, row_blk0=0, n_row_blks=NP_TOK // tm,
                      col_map=lambda j: j + n_att_blks, n_col_blks=n_att_blks,
                      tm=tm, tn=tn, out_dtype=F32, name="w_in_k_prompt")
    newv = _ws_matmul(h, w_in0, row_blk0=0, n_row_blks=NP_TOK // tm,
                      col_map=lambda j: j + 2 * n_att_blks, n_col_blks=n_att_blks,
                      tm=tm, tn=tn, out_dtype=F32, name="w_in_v_prompt")
    kv_s = _ws_matmul(h, w_in0, row_blk0=NP_TOK // tm, n_row_blks=NS_TOK // tm,
                      col_map=lambda j: j + n_att_blks, n_col_blks=2 * n_att_blks,
                      tm=tm, tn=tn, out_dtype=BF16, name="w_in_kv_sample")

    att_p = _ctx_attention(qu, newk, newv)
    cls_of_g, patterns = _nbr_classes()
    bias = _rpb_bias(rpb[0].reshape(-1), patterns)
    ck = cache_k[:, 0].reshape(DEC_BATCH * PAST_LEN, D_ATT)
    cv = cache_v[:, 0].reshape(DEC_BATCH * PAST_LEN, D_ATT)
    att_s = _nbr_attention(qu, kv_s, ck, cv, bias, cls_of_g)

    cc, sc = _dft_tables(FGROUP_DIM)
    w1 = jnp.asarray(np.concatenate([cc, -sc], axis=1)).astype(BF16)
    ab = _dft_chan(qu, w1)
    ctp, stp = _dft_tables(SEQ)
    fou_p = _dft_pos_prompt(ab, jnp.asarray(ctp).astype(BF16), jnp.asarray(stp).astype(BF16))
    cts, sts = _dft_tables(DEC_SEQ)
    fou_s = _dft_pos_sample(ab, jnp.asarray(cts).astype(BF16), jnp.asarray(sts).astype(BF16))

    x1 = _wout(att_p, att_s, fou_p, fou_s, w_out[0], xp, xs, mod3)

    h2 = _norm2(x1, norm2_g[0][None, :], mod3)
    a = _gate_up(h2, w_gate[0], w_up[0])
    wd = _cast_pad_wdown(w_down[0])
    fg = final_g[None, :]
    y_prompt = _down(a, wd, x1, mod3, fg, row0=0, n_rows=NP_TOK,
                     name="ffn_down_prompt").reshape(BATCH, SEQ, D_MODEL)
    y_sample = _down(a, wd, x1, mod3, fg, row0=NP_TOK, n_rows=NS_TOK,
                     name="ffn_down_sample").reshape(DEC_BATCH, DEC_SEQ, D_MODEL)
    new_cache_k = newk.reshape(BATCH, 1, SEQ, N_HEADS, HEAD_DIM)
    new_cache_v = newv.reshape(BATCH, 1, SEQ, N_HEADS, HEAD_DIM)
    return (y_prompt, y_sample, new_cache_k, new_cache_v)
```

```python
import functools

import numpy as np
import jax
import jax.numpy as jnp
from jax import lax
from jax.experimental import pallas as pl
from jax.experimental.pallas import tpu as pltpu

F32 = jnp.float32
BF16 = jnp.bfloat16

D_MODEL = 4096
BATCH = 32
SEQ = 256
DEC_BATCH = 4
DEC_SEQ = 2048
PAST_LEN = 256
GRID_W = 64
GRID_ROWS = DEC_SEQ // GRID_W
D_ATT = 2048
D_FOURIER = 2048
HEAD_DIM = 128
N_HEADS = 16
N_FGROUPS = 4
FGROUP_DIM = 512
WIN_H = 8
WIN_W = 16
D_FF = 11008
D_FF_PAD = 11264
EPS = 1e-6
NEG_INF = -1e30
SCALE = HEAD_DIM ** -0.5

NP_TOK = BATCH * SEQ
NS_TOK = DEC_BATCH * DEC_SEQ
N_TOK = NP_TOK + NS_TOK
N_MOD = 8

VMEM_LIMIT = 56 * 1024 * 1024


def _params(n_axes, vmem=VMEM_LIMIT):
    return pltpu.CompilerParams(dimension_semantics=("arbitrary",) * n_axes,
                                vmem_limit_bytes=vmem)


def _mod_row(i, tm):
    nh = NP_TOK // tm
    return jnp.where(i < nh, 0, 1 + (i - nh) // (DEC_SEQ // tm))


def _ada_kernel(c_ref, w_ref, b_ref, o_ref):
    c = c_ref[...]
    s = (c * jax.nn.sigmoid(c)).astype(BF16)
    o_ref[...] = jnp.dot(s, w_ref[...].astype(BF16), preferred_element_type=F32) + b_ref[...]


def _ada(cvec, w_ada, b_ada):
    tn = 512
    n = w_ada.shape[1]
    return pl.pallas_call(
        _ada_kernel,
        grid=(n // tn,),
        in_specs=[pl.BlockSpec((N_MOD, D_MODEL), lambda j: (0, 0)),
                  pl.BlockSpec((D_MODEL, tn), lambda j: (0, j)),
                  pl.BlockSpec((1, tn), lambda j: (0, j))],
        out_specs=pl.BlockSpec((N_MOD, tn), lambda j: (0, j)),
        out_shape=jax.ShapeDtypeStruct((N_MOD, n), F32),
        compiler_params=_params(1),
        name="ada_mod",
    )(cvec, w_ada, b_ada)


ROW_CHUNK = 8


STATS_UNROLL = 8
APPLY_ROWS = 64
APPLY_COLS = 512


def _row_chunks(n_rows, fn):
    def body(c, carry):
        fn(pl.ds(pl.multiple_of(c * ROW_CHUNK, ROW_CHUNK), ROW_CHUNK))
        return carry
    lax.fori_loop(0, n_rows // ROW_CHUNK, body, 0, unroll=STATS_UNROLL)


def _row_col_blocks(n_rows, n_cols, fn):
    def body(c, carry):
        rows = pl.ds(pl.multiple_of(c * APPLY_ROWS, APPLY_ROWS), APPLY_ROWS)
        for j in range(n_cols // APPLY_COLS):
            fn(rows, slice(j * APPLY_COLS, (j + 1) * APPLY_COLS))
        return carry
    lax.fori_loop(0, n_rows // APPLY_ROWS, body, 0)


def _norm_mod(x_ref, g_ref, sh_ref, sc_ref, o_ref, r_ref, gs_ref):
    gs_ref[...] = g_ref[...] * (1.0 + sc_ref[0])

    def stats(rows):
        x = x_ref[rows, :]
        r_ref[rows, :] = lax.rsqrt(jnp.mean(x * x, axis=-1, keepdims=True) + EPS)
    _row_chunks(o_ref.shape[0], stats)

    def apply(rows, cols):
        o_ref[rows, cols] = (x_ref[rows, cols] * r_ref[rows, :] * gs_ref[:, cols]
                             + sh_ref[0, :, cols]).astype(o_ref.dtype)
    _row_col_blocks(o_ref.shape[0], o_ref.shape[1], apply)


def _norm_mod2_kernel(xp_ref, xs_ref, g_ref, sh_ref, sc_ref, o_ref, r_ref, gs_ref, *, nh):
    i = pl.program_id(0)

    @pl.when(i < nh)
    def _():
        _norm_mod(xp_ref, g_ref, sh_ref, sc_ref, o_ref, r_ref, gs_ref)

    @pl.when(i >= nh)
    def _():
        _norm_mod(xs_ref, g_ref, sh_ref, sc_ref, o_ref, r_ref, gs_ref)


def _mod_spec(part, tm):
    return pl.BlockSpec((1, 1, D_MODEL), lambda i: (_mod_row(i, tm) * 6 + part, 0, 0))


def _norm_modulate(xp, xs, g, mod3, *, shift_part, scale_part, name):
    tm = 256
    nh = NP_TOK // tm
    return pl.pallas_call(
        functools.partial(_norm_mod2_kernel, nh=nh),
        grid=(N_TOK // tm,),
        in_specs=[pl.BlockSpec((tm, D_MODEL), lambda i: (jnp.minimum(i, nh - 1), 0)),
                  pl.BlockSpec((tm, D_MODEL), lambda i: (jnp.maximum(i - nh, 0), 0)),
                  pl.BlockSpec((1, D_MODEL), lambda i: (0, 0)),
                  _mod_spec(shift_part, tm), _mod_spec(scale_part, tm)],
        out_specs=pl.BlockSpec((tm, D_MODEL), lambda i: (i, 0)),
        out_shape=jax.ShapeDtypeStruct((N_TOK, D_MODEL), BF16),
        scratch_shapes=[pltpu.VMEM((tm, 1), F32), pltpu.VMEM((1, D_MODEL), F32)],
        compiler_params=_params(1),
        name=name,
    )(xp, xs, g, mod3, mod3)


CAST_ROWS = 512


def _cast_weight(w_ref, wb_ref):
    def body(c, carry):
        r = pl.multiple_of(c * CAST_ROWS, CAST_ROWS)
        wb_ref[pl.ds(r, CAST_ROWS), :] = w_ref[pl.ds(r, CAST_ROWS), :].astype(BF16)
        return carry
    lax.fori_loop(0, w_ref.shape[0] // CAST_ROWS, body, 0)


def _ws_kernel(x_ref, w_ref, o_ref, wb_ref):
    @pl.when(pl.program_id(1) == 0)
    def _():
        _cast_weight(w_ref, wb_ref)

    o_ref[...] = jnp.dot(x_ref[...], wb_ref[...],
                         preferred_element_type=F32).astype(o_ref.dtype)


def _ws_matmul(x, w, *, row_blk0, n_row_blks, col_map, n_col_blks, tm, tn, out_dtype, name):
    k = x.shape[1]
    return pl.pallas_call(
        _ws_kernel,
        grid=(n_col_blks, n_row_blks),
        in_specs=[pl.BlockSpec((tm, k), lambda j, i: (row_blk0 + i, 0)),
                  pl.BlockSpec((k, tn), lambda j, i: (0, col_map(j)))],
        out_specs=pl.BlockSpec((tm, tn), lambda j, i: (i, j)),
        out_shape=jax.ShapeDtypeStruct((n_row_blks * tm, n_col_blks * tn), out_dtype),
        scratch_shapes=[pltpu.VMEM((k, tn), BF16)],
        compiler_params=_params(2),
        name=name,
    )(x, w)


def _ctx_attn_kernel(q_ref, k_ref, v_ref, o_ref):
    for h in range(N_HEADS):
        hs = slice(h * HEAD_DIM, (h + 1) * HEAD_DIM)
        q = q_ref[:, hs]
        k = k_ref[:, hs].astype(BF16)
        v = v_ref[:, hs].astype(BF16)
        s = lax.dot_general(q, k, (((1,), (1,)), ((), ())), preferred_element_type=F32) * SCALE
        m = jnp.max(s, axis=-1, keepdims=True)
        p = jnp.exp(s - m)
        l = jnp.sum(p, axis=-1, keepdims=True)
        o = jnp.dot(p.astype(BF16), v, preferred_element_type=F32) / l
        o_ref[:, hs] = o.astype(o_ref.dtype)


def _ctx_attention(qu, newk, newv):
    return pl.pallas_call(
        _ctx_attn_kernel,
        grid=(BATCH,),
        in_specs=[pl.BlockSpec((SEQ, D_ATT), lambda b: (b, 0)),
                  pl.BlockSpec((SEQ, D_ATT), lambda b: (b, 0)),
                  pl.BlockSpec((SEQ, D_ATT), lambda b: (b, 0))],
        out_specs=pl.BlockSpec((SEQ, D_ATT), lambda b: (b, 0)),
        out_shape=jax.ShapeDtypeStruct((NP_TOK, D_ATT), BF16),
        compiler_params=_params(1),
        name="ctx_attention",
    )(qu, newk, newv)


NA_QROWS = 4
NA_KROWS = NA_QROWS + WIN_H
NA_Q = NA_QROWS * GRID_W
NA_K = NA_KROWS * GRID_W
NA_GROUPS = GRID_ROWS // NA_QROWS


def _nbr_window_start(g):
    lo, hi = 0, GRID_ROWS - NA_KROWS
    if isinstance(g, int):
        return min(max(NA_QROWS * g - WIN_H // 2, lo), hi)
    return jnp.clip(NA_QROWS * g - WIN_H // 2, lo, hi)


def _nbr_classes():
    patterns, cls_of_g = [], []
    for g in range(NA_GROUPS):
        start = _nbr_window_start(g)
        pat = []
        for i in range(NA_QROWS):
            r = NA_QROWS * g + i
            rstart = min(max(r - WIN_H // 2, 0), GRID_ROWS - WIN_H)
            pat.append(tuple((start + j - r + WIN_H - 1) if rstart <= start + j < rstart + WIN_H else None
                             for j in range(NA_KROWS)))
        pat = tuple(pat)
        if pat not in patterns:
            patterns.append(pat)
        cls_of_g.append(patterns.index(pat))
    return tuple(cls_of_g), tuple(patterns)


def _rpb_bias_kernel(rpb_ref, o_ref, t_ref, *, patterns):
    h = pl.program_id(0)
    qc = lax.broadcasted_iota(jnp.int32, (GRID_W, GRID_W), 0)
    kc = lax.broadcasted_iota(jnp.int32, (GRID_W, GRID_W), 1)
    dc = jnp.clip(kc - qc + (WIN_W - 1), 0, 2 * WIN_W - 2)
    cstart = jnp.clip(qc - WIN_W // 2, 0, GRID_W - WIN_W)
    mask = (kc >= cstart) & (kc < cstart + WIN_W)
    n_dc = 2 * WIN_W - 1
    n_dr = 2 * WIN_H - 1
    for dr in range(n_dr):
        t = jnp.zeros((GRID_W, GRID_W), F32)
        for d in range(n_dc):
            t = jnp.where(dc == d, rpb_ref[h * (n_dr * n_dc) + dr * n_dc + d], t)
        t_ref[dr] = jnp.where(mask, t, NEG_INF)
    outside = jnp.full((GRID_W, GRID_W), NEG_INF, F32)
    for c, pat in enumerate(patterns):
        for i in range(NA_QROWS):
            for j in range(NA_KROWS):
                dr = pat[i][j]
                o_ref[c, 0, i * GRID_W:(i + 1) * GRID_W, j * GRID_W:(j + 1) * GRID_W] = (
                    outside if dr is None else t_ref[dr])


def _rpb_bias(rpb_flat, patterns):
    n_cls = len(patterns)
    return pl.pallas_call(
        functools.partial(_rpb_bias_kernel, patterns=patterns),
        grid=(N_HEADS,),
        in_specs=[pl.BlockSpec(memory_space=pltpu.SMEM)],
        out_specs=pl.BlockSpec((n_cls, 1, NA_Q, NA_K), lambda h: (0, h, 0, 0)),
        out_shape=jax.ShapeDtypeStruct((n_cls, N_HEADS, NA_Q, NA_K), F32),
        scratch_shapes=[pltpu.VMEM((2 * WIN_H - 1, GRID_W, GRID_W), F32)],
        compiler_params=_params(1),
        name="rpb_bias",
    )(rpb_flat)


NA_HEADS = 4
NA_COLS = NA_HEADS * HEAD_DIM


def _nbr_attn_kernel(q_ref, k_ref, v_ref, kc_ref, vc_ref, bias_ref, o_ref, kcb_ref, vcb_ref, *,
                     cls_of_g):
    kcb_ref[...] = kc_ref[...].astype(BF16)
    vcb_ref[...] = vc_ref[...].astype(BF16)
    dn = (((1,), (1,)), ((), ()))

    def group_body(g, carry):
        cls = jnp.int32(cls_of_g[0])
        for gg in range(1, NA_GROUPS):
            if cls_of_g[gg] != cls_of_g[gg - 1]:
                cls = jnp.where(g >= gg, cls_of_g[gg], cls)
        q0 = pl.multiple_of(g * NA_Q, NA_Q)
        k0 = pl.multiple_of(_nbr_window_start(g) * GRID_W, GRID_W)
        for h in range(NA_HEADS):
            hs = slice(h * HEAD_DIM, (h + 1) * HEAD_DIM)
            q = q_ref[pl.ds(q0, NA_Q), hs]
            kw = k_ref[pl.ds(k0, NA_K), hs]
            vw = v_ref[pl.ds(k0, NA_K), hs]
            s_loc = lax.dot_general(q, kw, dn, preferred_element_type=F32) * SCALE + bias_ref[cls, h]
            s_ctx = lax.dot_general(q, kcb_ref[:, hs], dn, preferred_element_type=F32) * SCALE
            m = jnp.maximum(jnp.max(s_loc, axis=-1, keepdims=True),
                            jnp.max(s_ctx, axis=-1, keepdims=True))
            p_loc = jnp.exp(s_loc - m)
            p_ctx = jnp.exp(s_ctx - m)
            l = jnp.sum(p_loc, axis=-1, keepdims=True) + jnp.sum(p_ctx, axis=-1, keepdims=True)
            o = (jnp.dot(p_loc.astype(BF16), vw, preferred_element_type=F32)
                 + jnp.dot(p_ctx.astype(BF16), vcb_ref[:, hs], preferred_element_type=F32)) / l
            o_ref[pl.ds(q0, NA_Q), hs] = o.astype(o_ref.dtype)
        return carry

    lax.fori_loop(0, NA_GROUPS, group_body, 0)


def _nbr_attention(qu, kv_s, ck, cv, bias, cls_of_g):
    n_hg = N_HEADS // NA_HEADS
    n_cls = bias.shape[0]
    row_blk0 = NP_TOK // DEC_SEQ
    return pl.pallas_call(
        functools.partial(_nbr_attn_kernel, cls_of_g=cls_of_g),
        grid=(n_hg, DEC_BATCH),
        in_specs=[pl.BlockSpec((DEC_SEQ, NA_COLS), lambda g, b: (row_blk0 + b, g)),
                  pl.BlockSpec((DEC_SEQ, NA_COLS), lambda g, b: (b, g)),
                  pl.BlockSpec((DEC_SEQ, NA_COLS), lambda g, b: (b, n_hg + g)),
                  pl.BlockSpec((PAST_LEN, NA_COLS), lambda g, b: (b, g)),
                  pl.BlockSpec((PAST_LEN, NA_COLS), lambda g, b: (b, g)),
                  pl.BlockSpec((n_cls, NA_HEADS, NA_Q, NA_K), lambda g, b: (0, g, 0, 0))],
        out_specs=pl.BlockSpec((DEC_SEQ, NA_COLS), lambda g, b: (b, g)),
        out_shape=jax.ShapeDtypeStruct((NS_TOK, D_ATT), BF16),
        scratch_shapes=[pltpu.VMEM((PAST_LEN, NA_COLS), BF16),
                        pltpu.VMEM((PAST_LEN, NA_COLS), BF16)],
        compiler_params=_params(2),
        name="nbr_attention",
    )(qu, kv_s, kv_s, ck, cv, bias)


def _dft_tables(n):
    idx = np.arange(n, dtype=np.int64)
    ang = (2.0 * np.pi / n) * ((idx[:, None] * idx[None, :]) % n).astype(np.float64)
    return ((np.cos(ang) / np.sqrt(n)).astype(np.float32),
            (-np.sin(ang) / np.sqrt(n)).astype(np.float32))


def _dft_chan_kernel(u_ref, w_ref, o_ref):
    for g in range(N_FGROUPS):
        u = u_ref[:, g * FGROUP_DIM:(g + 1) * FGROUP_DIM]
        o_ref[:, g * 2 * FGROUP_DIM:(g + 1) * 2 * FGROUP_DIM] = jnp.dot(
            u, w_ref[...], preferred_element_type=F32).astype(o_ref.dtype)


def _dft_chan(qu, w1):
    tm = 1024
    return pl.pallas_call(
        _dft_chan_kernel,
        grid=(N_TOK // tm,),
        in_specs=[pl.BlockSpec((tm, D_FOURIER), lambda i: (i, 1)),
                  pl.BlockSpec((FGROUP_DIM, 2 * FGROUP_DIM), lambda i: (0, 0))],
        out_specs=pl.BlockSpec((tm, 2 * D_FOURIER), lambda i: (i, 0)),
        out_shape=jax.ShapeDtypeStruct((N_TOK, 2 * D_FOURIER), BF16),
        compiler_params=_params(1),
        name="dft_channels",
    )(qu, w1)


def _dft_pos_kernel(ct_ref, st_ref, ab_ref, o_ref):
    for g in range(ab_ref.shape[1] // (2 * FGROUP_DIM)):
        a = ab_ref[:, g * 2 * FGROUP_DIM:g * 2 * FGROUP_DIM + FGROUP_DIM]
        b = ab_ref[:, g * 2 * FGROUP_DIM + FGROUP_DIM:(g + 1) * 2 * FGROUP_DIM]
        o = (jnp.dot(ct_ref[...], a, preferred_element_type=F32)
             + jnp.dot(st_ref[...], b, preferred_element_type=F32))
        o_ref[:, g * FGROUP_DIM:(g + 1) * FGROUP_DIM] = o.astype(o_ref.dtype)


def _dft_pos_prompt(ab, ct, st):
    return pl.pallas_call(
        _dft_pos_kernel,
        grid=(BATCH,),
        in_specs=[pl.BlockSpec((SEQ, SEQ), lambda b: (0, 0)),
                  pl.BlockSpec((SEQ, SEQ), lambda b: (0, 0)),
                  pl.BlockSpec((SEQ, 2 * D_FOURIER), lambda b: (b, 0))],
        out_specs=pl.BlockSpec((SEQ, D_FOURIER), lambda b: (b, 0)),
        out_shape=jax.ShapeDtypeStruct((NP_TOK, D_FOURIER), BF16),
        compiler_params=_params(1),
        name="dft_pos_prompt",
    )(ct, st, ab)


def _dft_pos_sample(ab, ct, st):
    tr = 512
    row_blk0 = NP_TOK // DEC_SEQ
    return pl.pallas_call(
        _dft_pos_kernel,
        grid=(DEC_BATCH, N_FGROUPS, DEC_SEQ // tr),
        in_specs=[pl.BlockSpec((tr, DEC_SEQ), lambda b, g, t: (t, 0)),
                  pl.BlockSpec((tr, DEC_SEQ), lambda b, g, t: (t, 0)),
                  pl.BlockSpec((DEC_SEQ, 2 * FGROUP_DIM), lambda b, g, t: (row_blk0 + b, g))],
        out_specs=pl.BlockSpec((tr, FGROUP_DIM), lambda b, g, t: (b * (DEC_SEQ // tr) + t, g)),
        out_shape=jax.ShapeDtypeStruct((NS_TOK, D_FOURIER), BF16),
        compiler_params=_params(3),
        name="dft_pos_sample",
    )(ct, st, ab)


def _wout_kernel(a_ref, f_ref, w_ref, x_ref, g_ref, o_ref, wb_ref):
    @pl.when(pl.program_id(1) == 0)
    def _():
        _cast_weight(w_ref, wb_ref)

    acc = (jnp.dot(a_ref[...], wb_ref[0:D_ATT, :], preferred_element_type=F32)
           + jnp.dot(f_ref[...], wb_ref[D_ATT:D_ATT + D_FOURIER, :], preferred_element_type=F32))
    o_ref[...] = x_ref[...] + g_ref[0] * acc


def _wout(att, fou, w_out, x, mod3, *, row0, name):
    tm, tn = 1024, 512
    n_rows = x.shape[0]
    blk0 = row0 // tm
    return pl.pallas_call(
        _wout_kernel,
        grid=(D_MODEL // tn, n_rows // tm),
        in_specs=[pl.BlockSpec((tm, D_ATT), lambda j, i: (i, 0)),
                  pl.BlockSpec((tm, D_FOURIER), lambda j, i: (i, 0)),
                  pl.BlockSpec((D_MODEL, tn), lambda j, i: (0, j)),
                  pl.BlockSpec((tm, tn), lambda j, i: (i, j)),
                  pl.BlockSpec((1, 1, tn), lambda j, i: (_mod_row(blk0 + i, tm) * 6 + 2, 0, j))],
        out_specs=pl.BlockSpec((tm, tn), lambda j, i: (i, j)),
        out_shape=jax.ShapeDtypeStruct((n_rows, D_MODEL), F32),
        scratch_shapes=[pltpu.VMEM((D_MODEL, tn), BF16)],
        compiler_params=_params(2),
        name=name,
    )(att, fou, w_out, x, mod3)


FF_TN = 256


def _gate_up_kernel(h_ref, wg_ref, wu_ref, o_ref, wgb_ref, wub_ref, *, n_real):
    j = pl.program_id(0)

    @pl.when((pl.program_id(1) == 0) & (j < n_real))
    def _():
        _cast_weight(wg_ref, wgb_ref)
        _cast_weight(wu_ref, wub_ref)

    @pl.when(j < n_real)
    def _():
        for c in range(h_ref.shape[0] // FF_ROWS):
            rs = slice(c * FF_ROWS, (c + 1) * FF_ROWS)
            h = h_ref[rs, :]
            g = jnp.dot(h, wgb_ref[...], preferred_element_type=F32)
            u = jnp.dot(h, wub_ref[...], preferred_element_type=F32)
            o_ref[rs, :] = (g * jax.nn.sigmoid(g) * u).astype(o_ref.dtype)

    @pl.when(j >= n_real)
    def _():
        o_ref[...] = jnp.zeros_like(o_ref)


FF_ROWS = 512


def _gate_up(h2, w_gate, w_up):
    tm, tn = 2048, FF_TN
    n_real = D_FF // tn
    wmap = lambda j, i: (0, jnp.minimum(j, n_real - 1))
    return pl.pallas_call(
        functools.partial(_gate_up_kernel, n_real=n_real),
        grid=(D_FF_PAD // tn, N_TOK // tm),
        in_specs=[pl.BlockSpec((tm, D_MODEL), lambda j, i: (i, 0)),
                  pl.BlockSpec((D_MODEL, tn), wmap),
                  pl.BlockSpec((D_MODEL, tn), wmap)],
        out_specs=pl.BlockSpec((tm, tn), lambda j, i: (i, j)),
        out_shape=jax.ShapeDtypeStruct((N_TOK, D_FF_PAD), BF16),
        scratch_shapes=[pltpu.VMEM((D_MODEL, tn), BF16), pltpu.VMEM((D_MODEL, tn), BF16)],
        compiler_params=_params(2),
        name="ffn_gate_up",
    )(h2, w_gate, w_up)


def _cast_pad_kernel(w_ref, o_ref, *, n_real):
    r = pl.program_id(0)

    @pl.when(r < n_real)
    def _():
        o_ref[...] = w_ref[...].astype(o_ref.dtype)

    @pl.when(r >= n_real)
    def _():
        o_ref[...] = jnp.zeros_like(o_ref)


def _cast_pad_wdown(w_down):
    tr = 256
    n_real = D_FF // tr
    return pl.pallas_call(
        functools.partial(_cast_pad_kernel, n_real=n_real),
        grid=(D_FF_PAD // tr,),
        in_specs=[pl.BlockSpec((tr, D_MODEL), lambda r: (jnp.minimum(r, n_real - 1), 0))],
        out_specs=pl.BlockSpec((tr, D_MODEL), lambda r: (r, 0)),
        out_shape=jax.ShapeDtypeStruct((D_FF_PAD, D_MODEL), BF16),
        compiler_params=_params(1),
        name="w_down_cast",
    )(w_down)


def _down_kernel(a_ref, w_ref, x_ref, g_ref, fg_ref, o_ref, r_ref, *, nk):
    k = pl.program_id(1)

    @pl.when(k == 0)
    def _():
        o_ref[...] = jnp.zeros_like(o_ref)

    for n in range(D_MODEL // DOWN_TN):
        ns = slice(n * DOWN_TN, (n + 1) * DOWN_TN)
        o_ref[:, ns] += jnp.dot(a_ref[...], w_ref[:, ns], preferred_element_type=F32)

    @pl.when(k == nk - 1)
    def _():
        def residual_stats(rows):
            x2 = x_ref[rows, :] + g_ref[0] * o_ref[rows, :]
            o_ref[rows, :] = x2
            r_ref[rows, :] = lax.rsqrt(jnp.mean(x2 * x2, axis=-1, keepdims=True) + EPS)
        _row_chunks(o_ref.shape[0], residual_stats)

        def normalise(rows, cols):
            o_ref[rows, cols] = o_ref[rows, cols] * r_ref[rows, :] * fg_ref[:, cols]
        _row_col_blocks(o_ref.shape[0], o_ref.shape[1], normalise)


DOWN_TN = 512


def _down(a, wd, x1, mod3, final_g, *, row0, name):
    tm, tk = 512, 1024
    nk = D_FF_PAD // tk
    n_rows = x1.shape[0]
    blk0 = row0 // tm
    return pl.pallas_call(
        functools.partial(_down_kernel, nk=nk),
        grid=(n_rows // tm, nk),
        in_specs=[pl.BlockSpec((tm, tk), lambda i, k: (blk0 + i, k)),
                  pl.BlockSpec((tk, D_MODEL), lambda i, k: (k, 0)),
                  pl.BlockSpec((tm, D_MODEL), lambda i, k: (i, 0)),
                  pl.BlockSpec((1, 1, D_MODEL), lambda i, k: (_mod_row(blk0 + i, tm) * 6 + 5, 0, 0)),
                  pl.BlockSpec((1, D_MODEL), lambda i, k: (0, 0))],
        out_specs=pl.BlockSpec((tm, D_MODEL), lambda i, k: (i, 0)),
        out_shape=jax.ShapeDtypeStruct((n_rows, D_MODEL), F32),
        scratch_shapes=[pltpu.VMEM((tm, 1), F32)],
        compiler_params=_params(2),
        name=name,
    )(a, wd, x1, mod3, final_g)


def kernel(x_prompt, x_sample, cache_k, cache_v, c, c_ctx, w_ada, b_ada, norm1_g, w_in, rpb,
           w_out, norm2_g, w_gate, w_up, w_down, final_g):
    xp = x_prompt.reshape(NP_TOK, D_MODEL)
    xs = x_sample.reshape(NS_TOK, D_MODEL)

    cvec = jnp.concatenate([c_ctx[None, :], c, jnp.zeros((N_MOD - 1 - DEC_BATCH, D_MODEL), F32)], axis=0)
    mod = _ada(cvec, w_ada[0], b_ada[0][None, :])
    mod3 = mod.reshape(N_MOD * 6, 1, D_MODEL)

    h = _norm_modulate(xp, xs, norm1_g[0][None, :], mod3, shift_part=0, scale_part=1,
                       name="norm1_mod")

    w_in0 = w_in[0]
    tm, tn = 1024, 512
    n_att_blks = D_ATT // tn
    qu = _ws_matmul(h, w_in0, row_blk0=0, n_row_blks=N_TOK // tm,
                    col_map=lambda j: jnp.where(j < n_att_blks, j, j + 2 * n_att_blks),
                    n_col_blks=2 * n_att_blks, tm=tm, tn=tn, out_dtype=BF16, name="w_in_q_u")
    newk = _ws_matmul(h, w_in0, row_blk0=0, n_row_blks=NP_TOK // tm,
                      col_map=lambda j: j + n_att_blks, n_col_blks=n_att_blks,
                      tm=tm, tn=tn, out_dtype=F32, name="w_in_k_prompt")
    newv = _ws_matmul(h, w_in0, row_blk0=0, n_row_blks=NP_TOK // tm,
                      col_map=lambda j: j + 2 * n_att_blks, n_col_blks=n_att_blks,
                      tm=tm, tn=tn, out_dtype=F32, name="w_in_v_prompt")
    kv_s = _ws_matmul(h, w_in0, row_blk0=NP_TOK // tm, n_row_blks=NS_TOK // tm,
                      col_map=lambda j: j + n_att_blks, n_col_blks=2 * n_att_blks,
                      tm=tm, tn=tn, out_dtype=BF16, name="w_in_kv_sample")

    att_p = _ctx_attention(qu, newk, newv)
    cls_of_g, patterns = _nbr_classes()
    bias = _rpb_bias(rpb[0].reshape(-1), patterns)
    ck = cache_k[:, 0].reshape(DEC_BATCH * PAST_LEN, D_ATT)
    cv = cache_v[:, 0].reshape(DEC_BATCH * PAST_LEN, D_ATT)
    att_s = _nbr_attention(qu, kv_s, ck, cv, bias, cls_of_g)

    cc, sc = _dft_tables(FGROUP_DIM)
    w1 = jnp.asarray(np.concatenate([cc, -sc], axis=1)).astype(BF16)
    ab = _dft_chan(qu, w1)
    ctp, stp = _dft_tables(SEQ)
    fou_p = _dft_pos_prompt(ab, jnp.asarray(ctp).astype(BF16), jnp.asarray(stp).astype(BF16))
    cts, sts = _dft_tables(DEC_SEQ)
    fou_s = _dft_pos_sample(ab, jnp.asarray(cts).astype(BF16), jnp.asarray(sts).astype(BF16))

    x1p = _wout(att_p, fou_p, w_out[0], xp, mod3, row0=0, name="w_out_prompt")
    x1s = _wout(att_s, fou_s, w_out[0], xs, mod3, row0=NP_TOK, name="w_out_sample")

    h2 = _norm_modulate(x1p, x1s, norm2_g[0][None, :], mod3, shift_part=3, scale_part=4,
                        name="norm2_mod")
    a = _gate_up(h2, w_gate[0], w_up[0])
    wd = _cast_pad_wdown(w_down[0])
    fg = final_g[None, :]
    y_prompt = _down(a, wd, x1p, mod3, fg, row0=0,
                     name="ffn_down_prompt").reshape(BATCH, SEQ, D_MODEL)
    y_sample = _down(a, wd, x1s, mod3, fg, row0=NP_TOK,
                     name="ffn_down_sample").reshape(DEC_BATCH, DEC_SEQ, D_MODEL)
    new_cache_k = newk.reshape(BATCH, 1, SEQ, N_HEADS, HEAD_DIM)
    new_cache_v = newv.reshape(BATCH, 1, SEQ, N_HEADS, HEAD_DIM)
    return (y_prompt, y_sample, new_cache_k, new_cache_v)
```

```python
import functools

import numpy as np
import jax
import jax.numpy as jnp
from jax import lax
from jax.experimental import pallas as pl
from jax.experimental.pallas import tpu as pltpu

F32 = jnp.float32
BF16 = jnp.bfloat16

D_MODEL = 4096
BATCH = 32
SEQ = 256
DEC_BATCH = 4
DEC_SEQ = 2048
PAST_LEN = 256
GRID_W = 64
GRID_ROWS = DEC_SEQ // GRID_W
D_ATT = 2048
D_FOURIER = 2048
HEAD_DIM = 128
N_HEADS = 16
N_FGROUPS = 4
FGROUP_DIM = 512
WIN_H = 8
WIN_W = 16
D_FF = 11008
D_FF_PAD = 11264
EPS = 1e-6
NEG_INF = -1e30
SCALE = HEAD_DIM ** -0.5

NP_TOK = BATCH * SEQ
NS_TOK = DEC_BATCH * DEC_SEQ
N_TOK = NP_TOK + NS_TOK
N_MOD = 8

VMEM_LIMIT = 56 * 1024 * 1024


def _params(n_axes, vmem=VMEM_LIMIT):
    return pltpu.CompilerParams(dimension_semantics=("arbitrary",) * n_axes,
                                vmem_limit_bytes=vmem)


def _mod_row(i, tm):
    nh = NP_TOK // tm
    return jnp.where(i < nh, 0, 1 + (i - nh) // (DEC_SEQ // tm))


def _ada_kernel(c_ref, w_ref, b_ref, o_ref):
    c = c_ref[...]
    s = (c * jax.nn.sigmoid(c)).astype(BF16)
    o_ref[...] = jnp.dot(s, w_ref[...].astype(BF16), preferred_element_type=F32) + b_ref[...]


def _ada(cvec, w_ada, b_ada):
    tn = 512
    n = w_ada.shape[1]
    return pl.pallas_call(
        _ada_kernel,
        grid=(n // tn,),
        in_specs=[pl.BlockSpec((N_MOD, D_MODEL), lambda j: (0, 0)),
                  pl.BlockSpec((D_MODEL, tn), lambda j: (0, j)),
                  pl.BlockSpec((1, tn), lambda j: (0, j))],
        out_specs=pl.BlockSpec((N_MOD, tn), lambda j: (0, j)),
        out_shape=jax.ShapeDtypeStruct((N_MOD, n), F32),
        compiler_params=_params(1),
        name="ada_mod",
    )(cvec, w_ada, b_ada)


ROW_CHUNK = 8


STATS_UNROLL = 8
APPLY_ROWS = 64
APPLY_COLS = 512


def _row_chunks(n_rows, fn):
    def body(c, carry):
        fn(pl.ds(pl.multiple_of(c * ROW_CHUNK, ROW_CHUNK), ROW_CHUNK))
        return carry
    lax.fori_loop(0, n_rows // ROW_CHUNK, body, 0, unroll=STATS_UNROLL)


def _row_col_blocks(n_rows, n_cols, fn):
    def body(c, carry):
        rows = pl.ds(pl.multiple_of(c * APPLY_ROWS, APPLY_ROWS), APPLY_ROWS)
        for j in range(n_cols // APPLY_COLS):
            fn(rows, slice(j * APPLY_COLS, (j + 1) * APPLY_COLS))
        return carry
    lax.fori_loop(0, n_rows // APPLY_ROWS, body, 0)


def _norm_mod(x_ref, g_ref, sh_ref, sc_ref, o_ref, r_ref, gs_ref):
    gs_ref[...] = g_ref[...] * (1.0 + sc_ref[0])

    def stats(rows):
        x = x_ref[rows, :]
        r_ref[rows, :] = lax.rsqrt(jnp.mean(x * x, axis=-1, keepdims=True) + EPS)
    _row_chunks(o_ref.shape[0], stats)

    def apply(rows, cols):
        o_ref[rows, cols] = (x_ref[rows, cols] * r_ref[rows, :] * gs_ref[:, cols]
                             + sh_ref[0, :, cols]).astype(o_ref.dtype)
    _row_col_blocks(o_ref.shape[0], o_ref.shape[1], apply)


def _norm_mod2_kernel(xp_ref, xs_ref, g_ref, sh_ref, sc_ref, o_ref, r_ref, gs_ref, *, nh):
    i = pl.program_id(0)

    @pl.when(i < nh)
    def _():
        _norm_mod(xp_ref, g_ref, sh_ref, sc_ref, o_ref, r_ref, gs_ref)

    @pl.when(i >= nh)
    def _():
        _norm_mod(xs_ref, g_ref, sh_ref, sc_ref, o_ref, r_ref, gs_ref)


def _mod_spec(part, tm):
    return pl.BlockSpec((1, 1, D_MODEL), lambda i: (_mod_row(i, tm) * 6 + part, 0, 0))


def _norm_modulate(xp, xs, g, mod3, *, shift_part, scale_part, name):
    tm = 512
    nh = NP_TOK // tm
    return pl.pallas_call(
        functools.partial(_norm_mod2_kernel, nh=nh),
        grid=(N_TOK // tm,),
        in_specs=[pl.BlockSpec((tm, D_MODEL), lambda i: (jnp.minimum(i, nh - 1), 0)),
                  pl.BlockSpec((tm, D_MODEL), lambda i: (jnp.maximum(i - nh, 0), 0)),
                  pl.BlockSpec((1, D_MODEL), lambda i: (0, 0)),
                  _mod_spec(shift_part, tm), _mod_spec(scale_part, tm)],
        out_specs=pl.BlockSpec((tm, D_MODEL), lambda i: (i, 0)),
        out_shape=jax.ShapeDtypeStruct((N_TOK, D_MODEL), BF16),
        scratch_shapes=[pltpu.VMEM((tm, 1), F32), pltpu.VMEM((1, D_MODEL), F32)],
        compiler_params=_params(1),
        name=name,
    )(xp, xs, g, mod3, mod3)


CAST_ROWS = 512


def _cast_weight(w_ref, wb_ref):
    def body(c, carry):
        r = pl.multiple_of(c * CAST_ROWS, CAST_ROWS)
        wb_ref[pl.ds(r, CAST_ROWS), :] = w_ref[pl.ds(r, CAST_ROWS), :].astype(BF16)
        return carry
    lax.fori_loop(0, w_ref.shape[0] // CAST_ROWS, body, 0)


def _ws_kernel(x_ref, w_ref, o_ref, wb_ref):
    @pl.when(pl.program_id(1) == 0)
    def _():
        _cast_weight(w_ref, wb_ref)

    o_ref[...] = jnp.dot(x_ref[...], wb_ref[...],
                         preferred_element_type=F32).astype(o_ref.dtype)


def _ws_matmul(x, w, *, row_blk0, n_row_blks, col_map, n_col_blks, tm, tn, out_dtype, name):
    k = x.shape[1]
    return pl.pallas_call(
        _ws_kernel,
        grid=(n_col_blks, n_row_blks),
        in_specs=[pl.BlockSpec((tm, k), lambda j, i: (row_blk0 + i, 0)),
                  pl.BlockSpec((k, tn), lambda j, i: (0, col_map(j)))],
        out_specs=pl.BlockSpec((tm, tn), lambda j, i: (i, j)),
        out_shape=jax.ShapeDtypeStruct((n_row_blks * tm, n_col_blks * tn), out_dtype),
        scratch_shapes=[pltpu.VMEM((k, tn), BF16)],
        compiler_params=_params(2),
        name=name,
    )(x, w)


def _ctx_attn_kernel(q_ref, k_ref, v_ref, o_ref):
    for h in range(N_HEADS):
        hs = slice(h * HEAD_DIM, (h + 1) * HEAD_DIM)
        q = q_ref[:, hs]
        k = k_ref[:, hs].astype(BF16)
        v = v_ref[:, hs].astype(BF16)
        s = lax.dot_general(q, k, (((1,), (1,)), ((), ())), preferred_element_type=F32) * SCALE
        m = jnp.max(s, axis=-1, keepdims=True)
        p = jnp.exp(s - m)
        l = jnp.sum(p, axis=-1, keepdims=True)
        o = jnp.dot(p.astype(BF16), v, preferred_element_type=F32) / l
        o_ref[:, hs] = o.astype(o_ref.dtype)


def _ctx_attention(qu, newk, newv):
    return pl.pallas_call(
        _ctx_attn_kernel,
        grid=(BATCH,),
        in_specs=[pl.BlockSpec((SEQ, D_ATT), lambda b: (b, 0)),
                  pl.BlockSpec((SEQ, D_ATT), lambda b: (b, 0)),
                  pl.BlockSpec((SEQ, D_ATT), lambda b: (b, 0))],
        out_specs=pl.BlockSpec((SEQ, D_ATT), lambda b: (b, 0)),
        out_shape=jax.ShapeDtypeStruct((NP_TOK, D_ATT), BF16),
        compiler_params=_params(1),
        name="ctx_attention",
    )(qu, newk, newv)


NA_QROWS = 4
NA_KROWS = NA_QROWS + WIN_H
NA_Q = NA_QROWS * GRID_W
NA_K = NA_KROWS * GRID_W
NA_GROUPS = GRID_ROWS // NA_QROWS


def _nbr_window_start(g):
    lo, hi = 0, GRID_ROWS - NA_KROWS
    if isinstance(g, int):
        return min(max(NA_QROWS * g - WIN_H // 2, lo), hi)
    return jnp.clip(NA_QROWS * g - WIN_H // 2, lo, hi)


def _nbr_classes():
    patterns, cls_of_g = [], []
    for g in range(NA_GROUPS):
        start = _nbr_window_start(g)
        pat = []
        for i in range(NA_QROWS):
            r = NA_QROWS * g + i
            rstart = min(max(r - WIN_H // 2, 0), GRID_ROWS - WIN_H)
            pat.append(tuple((start + j - r + WIN_H - 1) if rstart <= start + j < rstart + WIN_H else None
                             for j in range(NA_KROWS)))
        pat = tuple(pat)
        if pat not in patterns:
            patterns.append(pat)
        cls_of_g.append(patterns.index(pat))
    return tuple(cls_of_g), tuple(patterns)


def _rpb_bias_kernel(rpb_ref, o_ref, t_ref, *, patterns):
    h = pl.program_id(0)
    qc = lax.broadcasted_iota(jnp.int32, (GRID_W, GRID_W), 0)
    kc = lax.broadcasted_iota(jnp.int32, (GRID_W, GRID_W), 1)
    dc = jnp.clip(kc - qc + (WIN_W - 1), 0, 2 * WIN_W - 2)
    cstart = jnp.clip(qc - WIN_W // 2, 0, GRID_W - WIN_W)
    mask = (kc >= cstart) & (kc < cstart + WIN_W)
    n_dc = 2 * WIN_W - 1
    n_dr = 2 * WIN_H - 1
    for dr in range(n_dr):
        t = jnp.zeros((GRID_W, GRID_W), F32)
        for d in range(n_dc):
            t = jnp.where(dc == d, rpb_ref[h * (n_dr * n_dc) + dr * n_dc + d], t)
        t_ref[dr] = jnp.where(mask, t, NEG_INF)
    outside = jnp.full((GRID_W, GRID_W), NEG_INF, F32)
    for c, pat in enumerate(patterns):
        for i in range(NA_QROWS):
            for j in range(NA_KROWS):
                dr = pat[i][j]
                o_ref[c, 0, i * GRID_W:(i + 1) * GRID_W, j * GRID_W:(j + 1) * GRID_W] = (
                    outside if dr is None else t_ref[dr])


def _rpb_bias(rpb_flat, patterns):
    n_cls = len(patterns)
    return pl.pallas_call(
        functools.partial(_rpb_bias_kernel, patterns=patterns),
        grid=(N_HEADS,),
        in_specs=[pl.BlockSpec(memory_space=pltpu.SMEM)],
        out_specs=pl.BlockSpec((n_cls, 1, NA_Q, NA_K), lambda h: (0, h, 0, 0)),
        out_shape=jax.ShapeDtypeStruct((n_cls, N_HEADS, NA_Q, NA_K), F32),
        scratch_shapes=[pltpu.VMEM((2 * WIN_H - 1, GRID_W, GRID_W), F32)],
        compiler_params=_params(1),
        name="rpb_bias",
    )(rpb_flat)


NA_HEADS = 4
NA_COLS = NA_HEADS * HEAD_DIM


def _nbr_attn_kernel(q_ref, k_ref, v_ref, kc_ref, vc_ref, bias_ref, o_ref, kcb_ref, vcb_ref, *,
                     cls_of_g):
    kcb_ref[...] = kc_ref[...].astype(BF16)
    vcb_ref[...] = vc_ref[...].astype(BF16)
    dn = (((1,), (1,)), ((), ()))

    def group_body(g, carry):
        cls = jnp.int32(cls_of_g[0])
        for gg in range(1, NA_GROUPS):
            if cls_of_g[gg] != cls_of_g[gg - 1]:
                cls = jnp.where(g >= gg, cls_of_g[gg], cls)
        q0 = pl.multiple_of(g * NA_Q, NA_Q)
        k0 = pl.multiple_of(_nbr_window_start(g) * GRID_W, GRID_W)
        for h in range(NA_HEADS):
            hs = slice(h * HEAD_DIM, (h + 1) * HEAD_DIM)
            q = q_ref[pl.ds(q0, NA_Q), hs]
            kw = k_ref[pl.ds(k0, NA_K), hs]
            vw = v_ref[pl.ds(k0, NA_K), hs]
            s_loc = lax.dot_general(q, kw, dn, preferred_element_type=F32) * SCALE + bias_ref[cls, h]
            s_ctx = lax.dot_general(q, kcb_ref[:, hs], dn, preferred_element_type=F32) * SCALE
            m = jnp.maximum(jnp.max(s_loc, axis=-1, keepdims=True),
                            jnp.max(s_ctx, axis=-1, keepdims=True))
            p_loc = jnp.exp(s_loc - m)
            p_ctx = jnp.exp(s_ctx - m)
            l = jnp.sum(p_loc, axis=-1, keepdims=True) + jnp.sum(p_ctx, axis=-1, keepdims=True)
            o = (jnp.dot(p_loc.astype(BF16), vw, preferred_element_type=F32)
                 + jnp.dot(p_ctx.astype(BF16), vcb_ref[:, hs], preferred_element_type=F32)) / l
            o_ref[pl.ds(q0, NA_Q), hs] = o.astype(o_ref.dtype)
        return carry

    lax.fori_loop(0, NA_GROUPS, group_body, 0)


def _nbr_attention(qu, kv_s, ck, cv, bias, cls_of_g):
    n_hg = N_HEADS // NA_HEADS
    n_cls = bias.shape[0]
    row_blk0 = NP_TOK // DEC_SEQ
    return pl.pallas_call(
        functools.partial(_nbr_attn_kernel, cls_of_g=cls_of_g),
        grid=(n_hg, DEC_BATCH),
        in_specs=[pl.BlockSpec((DEC_SEQ, NA_COLS), lambda g, b: (row_blk0 + b, g)),
                  pl.BlockSpec((DEC_SEQ, NA_COLS), lambda g, b: (b, g)),
                  pl.BlockSpec((DEC_SEQ, NA_COLS), lambda g, b: (b, n_hg + g)),
                  pl.BlockSpec((PAST_LEN, NA_COLS), lambda g, b: (b, g)),
                  pl.BlockSpec((PAST_LEN, NA_COLS), lambda g, b: (b, g)),
                  pl.BlockSpec((n_cls, NA_HEADS, NA_Q, NA_K), lambda g, b: (0, g, 0, 0))],
        out_specs=pl.BlockSpec((DEC_SEQ, NA_COLS), lambda g, b: (b, g)),
        out_shape=jax.ShapeDtypeStruct((NS_TOK, D_ATT), BF16),
        scratch_shapes=[pltpu.VMEM((PAST_LEN, NA_COLS), BF16),
                        pltpu.VMEM((PAST_LEN, NA_COLS), BF16)],
        compiler_params=_params(2),
        name="nbr_attention",
    )(qu, kv_s, kv_s, ck, cv, bias)


def _dft_tables(n):
    idx = np.arange(n, dtype=np.int64)
    ang = (2.0 * np.pi / n) * ((idx[:, None] * idx[None, :]) % n).astype(np.float64)
    return ((np.cos(ang) / np.sqrt(n)).astype(np.float32),
            (-np.sin(ang) / np.sqrt(n)).astype(np.float32))


def _dft_chan_kernel(u_ref, w_ref, o_ref):
    for g in range(N_FGROUPS):
        u = u_ref[:, g * FGROUP_DIM:(g + 1) * FGROUP_DIM]
        o_ref[:, g * 2 * FGROUP_DIM:(g + 1) * 2 * FGROUP_DIM] = jnp.dot(
            u, w_ref[...], preferred_element_type=F32).astype(o_ref.dtype)


def _dft_chan(qu, w1):
    tm = 1024
    return pl.pallas_call(
        _dft_chan_kernel,
        grid=(N_TOK // tm,),
        in_specs=[pl.BlockSpec((tm, D_FOURIER), lambda i: (i, 1)),
                  pl.BlockSpec((FGROUP_DIM, 2 * FGROUP_DIM), lambda i: (0, 0))],
        out_specs=pl.BlockSpec((tm, 2 * D_FOURIER), lambda i: (i, 0)),
        out_shape=jax.ShapeDtypeStruct((N_TOK, 2 * D_FOURIER), BF16),
        compiler_params=_params(1),
        name="dft_channels",
    )(qu, w1)


def _dft_pos_kernel(ct_ref, st_ref, ab_ref, o_ref):
    n_in = ct_ref.shape[1]
    n_out = ct_ref.shape[0]
    for s in range(ab_ref.shape[0] // n_in):
        for g in range(ab_ref.shape[1] // (2 * FGROUP_DIM)):
            rows = slice(s * n_in, (s + 1) * n_in)
            a = ab_ref[rows, g * 2 * FGROUP_DIM:g * 2 * FGROUP_DIM + FGROUP_DIM]
            b = ab_ref[rows, g * 2 * FGROUP_DIM + FGROUP_DIM:(g + 1) * 2 * FGROUP_DIM]
            o = (jnp.dot(ct_ref[...], a, preferred_element_type=F32)
                 + jnp.dot(st_ref[...], b, preferred_element_type=F32))
            o_ref[s * n_out:(s + 1) * n_out, g * FGROUP_DIM:(g + 1) * FGROUP_DIM] = o.astype(o_ref.dtype)


def _dft_pos_prompt(ab, ct, st):
    n_seq = 4
    rows = n_seq * SEQ
    return pl.pallas_call(
        _dft_pos_kernel,
        grid=(BATCH // n_seq,),
        in_specs=[pl.BlockSpec((SEQ, SEQ), lambda b: (0, 0)),
                  pl.BlockSpec((SEQ, SEQ), lambda b: (0, 0)),
                  pl.BlockSpec((rows, 2 * D_FOURIER), lambda b: (b, 0))],
        out_specs=pl.BlockSpec((rows, D_FOURIER), lambda b: (b, 0)),
        out_shape=jax.ShapeDtypeStruct((NP_TOK, D_FOURIER), BF16),
        compiler_params=_params(1),
        name="dft_pos_prompt",
    )(ct, st, ab)


def _dft_pos_sample(ab, ct, st):
    tr = 1024
    row_blk0 = NP_TOK // DEC_SEQ
    return pl.pallas_call(
        _dft_pos_kernel,
        grid=(DEC_BATCH, N_FGROUPS, DEC_SEQ // tr),
        in_specs=[pl.BlockSpec((tr, DEC_SEQ), lambda b, g, t: (t, 0)),
                  pl.BlockSpec((tr, DEC_SEQ), lambda b, g, t: (t, 0)),
                  pl.BlockSpec((DEC_SEQ, 2 * FGROUP_DIM), lambda b, g, t: (row_blk0 + b, g))],
        out_specs=pl.BlockSpec((tr, FGROUP_DIM), lambda b, g, t: (b * (DEC_SEQ // tr) + t, g)),
        out_shape=jax.ShapeDtypeStruct((NS_TOK, D_FOURIER), BF16),
        compiler_params=_params(3),
        name="dft_pos_sample",
    )(ct, st, ab)


def _wout_kernel(a_ref, f_ref, w_ref, x_ref, g_ref, o_ref, wb_ref):
    @pl.when(pl.program_id(1) == 0)
    def _():
        _cast_weight(w_ref, wb_ref)

    acc = (jnp.dot(a_ref[...], wb_ref[0:D_ATT, :], preferred_element_type=F32)
           + jnp.dot(f_ref[...], wb_ref[D_ATT:D_ATT + D_FOURIER, :], preferred_element_type=F32))
    o_ref[...] = x_ref[...] + g_ref[0] * acc


def _wout(att, fou, w_out, x, mod3, *, row0, name):
    tm, tn = 1024, 512
    n_rows = x.shape[0]
    blk0 = row0 // tm
    return pl.pallas_call(
        _wout_kernel,
        grid=(D_MODEL // tn, n_rows // tm),
        in_specs=[pl.BlockSpec((tm, D_ATT), lambda j, i: (i, 0)),
                  pl.BlockSpec((tm, D_FOURIER), lambda j, i: (i, 0)),
                  pl.BlockSpec((D_MODEL, tn), lambda j, i: (0, j)),
                  pl.BlockSpec((tm, tn), lambda j, i: (i, j)),
                  pl.BlockSpec((1, 1, tn), lambda j, i: (_mod_row(blk0 + i, tm) * 6 + 2, 0, j))],
        out_specs=pl.BlockSpec((tm, tn), lambda j, i: (i, j)),
        out_shape=jax.ShapeDtypeStruct((n_rows, D_MODEL), F32),
        scratch_shapes=[pltpu.VMEM((D_MODEL, tn), BF16)],
        compiler_params=_params(2),
        name=name,
    )(att, fou, w_out, x, mod3)


FF_TN = 256


def _gate_up_kernel(h_ref, wg_ref, wu_ref, wd_ref, o_ref, wdb_ref, wgb_ref, wub_ref, *, n_real):
    j = pl.program_id(0)

    @pl.when((pl.program_id(1) == 0) & (j < n_real))
    def _():
        _cast_weight(wg_ref, wgb_ref)
        _cast_weight(wu_ref, wub_ref)

    @pl.when(j < n_real)
    def _():
        for c in range(h_ref.shape[0] // FF_ROWS):
            rs = slice(c * FF_ROWS, (c + 1) * FF_ROWS)
            h = h_ref[rs, :]
            g = jnp.dot(h, wgb_ref[...], preferred_element_type=F32)
            u = jnp.dot(h, wub_ref[...], preferred_element_type=F32)
            o_ref[rs, :] = (g * jax.nn.sigmoid(g) * u).astype(o_ref.dtype)
        wdb_ref[...] = wd_ref[...].astype(wdb_ref.dtype)

    @pl.when(j >= n_real)
    def _():
        o_ref[...] = jnp.zeros_like(o_ref)
        wdb_ref[...] = jnp.zeros_like(wdb_ref)


FF_ROWS = 512


def _gate_up(h2, w_gate, w_up, w_down):
    tm, tn = 2048, FF_TN
    n_real = D_FF // tn
    ni = N_TOK // tm
    wd_rows = D_FF // (n_real * ni)
    assert wd_rows * n_real * ni == D_FF and wd_rows % 16 == 0
    assert (D_FF_PAD - D_FF) == (D_FF_PAD // tn - n_real) * ni * wd_rows
    wmap = lambda j, i: (0, jnp.minimum(j, n_real - 1))
    return pl.pallas_call(
        functools.partial(_gate_up_kernel, n_real=n_real),
        grid=(D_FF_PAD // tn, ni),
        in_specs=[pl.BlockSpec((tm, D_MODEL), lambda j, i: (i, 0)),
                  pl.BlockSpec((D_MODEL, tn), wmap),
                  pl.BlockSpec((D_MODEL, tn), wmap),
                  pl.BlockSpec((wd_rows, D_MODEL),
                               lambda j, i: (jnp.minimum(j * ni + i, n_real * ni - 1), 0))],
        out_specs=[pl.BlockSpec((tm, tn), lambda j, i: (i, j)),
                   pl.BlockSpec((wd_rows, D_MODEL), lambda j, i: (j * ni + i, 0))],
        out_shape=[jax.ShapeDtypeStruct((N_TOK, D_FF_PAD), BF16),
                   jax.ShapeDtypeStruct((D_FF_PAD, D_MODEL), BF16)],
        scratch_shapes=[pltpu.VMEM((D_MODEL, tn), BF16), pltpu.VMEM((D_MODEL, tn), BF16)],
        compiler_params=_params(2),
        name="ffn_gate_up",
    )(h2, w_gate, w_up, w_down)


def _down_kernel(a_ref, w_ref, x_ref, g_ref, fg_ref, o_ref, r_ref, *, nk):
    k = pl.program_id(1)

    @pl.when(k == 0)
    def _():
        o_ref[...] = jnp.zeros_like(o_ref)

    for n in range(D_MODEL // DOWN_TN):
        ns = slice(n * DOWN_TN, (n + 1) * DOWN_TN)
        o_ref[:, ns] += jnp.dot(a_ref[...], w_ref[:, ns], preferred_element_type=F32)

    @pl.when(k == nk - 1)
    def _():
        lanes = 128

        def residual_stats(c, carry):
            rows = pl.ds(pl.multiple_of(c * APPLY_ROWS, APPLY_ROWS), APPLY_ROWS)
            ss = jnp.zeros((APPLY_ROWS, lanes), F32)
            for j in range(D_MODEL // APPLY_COLS):
                cols = slice(j * APPLY_COLS, (j + 1) * APPLY_COLS)
                x2 = x_ref[rows, cols] + g_ref[0, :, cols] * o_ref[rows, cols]
                o_ref[rows, cols] = x2
                sq = x2 * x2
                for q in range(APPLY_COLS // lanes):
                    ss = ss + sq[:, q * lanes:(q + 1) * lanes]
            ms = jnp.sum(ss, axis=-1, keepdims=True) * (1.0 / D_MODEL)
            r_ref[rows, :] = lax.rsqrt(ms + EPS)
            return carry
        lax.fori_loop(0, o_ref.shape[0] // APPLY_ROWS, residual_stats, 0)

        def normalise(rows, cols):
            o_ref[rows, cols] = o_ref[rows, cols] * r_ref[rows, :] * fg_ref[:, cols]
        _row_col_blocks(o_ref.shape[0], o_ref.shape[1], normalise)


DOWN_TN = 512


def _down(a, wd, x1, mod3, final_g, *, row0, name):
    tm, tk = 512, 1024
    nk = D_FF_PAD // tk
    n_rows = x1.shape[0]
    blk0 = row0 // tm
    return pl.pallas_call(
        functools.partial(_down_kernel, nk=nk),
        grid=(n_rows // tm, nk),
        in_specs=[pl.BlockSpec((tm, tk), lambda i, k: (blk0 + i, k)),
                  pl.BlockSpec((tk, D_MODEL), lambda i, k: (k, 0)),
                  pl.BlockSpec((tm, D_MODEL), lambda i, k: (i, 0)),
                  pl.BlockSpec((1, 1, D_MODEL), lambda i, k: (_mod_row(blk0 + i, tm) * 6 + 5, 0, 0)),
                  pl.BlockSpec((1, D_MODEL), lambda i, k: (0, 0))],
        out_specs=pl.BlockSpec((tm, D_MODEL), lambda i, k: (i, 0)),
        out_shape=jax.ShapeDtypeStruct((n_rows, D_MODEL), F32),
        scratch_shapes=[pltpu.VMEM((tm, 1), F32)],
        compiler_params=_params(2),
        name=name,
    )(a, wd, x1, mod3, final_g)


def kernel(x_prompt, x_sample, cache_k, cache_v, c, c_ctx, w_ada, b_ada, norm1_g, w_in, rpb,
           w_out, norm2_g, w_gate, w_up, w_down, final_g):
    xp = x_prompt.reshape(NP_TOK, D_MODEL)
    xs = x_sample.reshape(NS_TOK, D_MODEL)

    cvec = jnp.concatenate([c_ctx[None, :], c, jnp.zeros((N_MOD - 1 - DEC_BATCH, D_MODEL), F32)], axis=0)
    mod = _ada(cvec, w_ada[0], b_ada[0][None, :])
    mod3 = mod.reshape(N_MOD * 6, 1, D_MODEL)

    h = _norm_modulate(xp, xs, norm1_g[0][None, :], mod3, shift_part=0, scale_part=1,
                       name="norm1_mod")

    w_in0 = w_in[0]
    tm, tn = 1024, 512
    n_att_blks = D_ATT // tn
    qu = _ws_matmul(h, w_in0, row_blk0=0, n_row_blks=N_TOK // tm,
                    col_map=lambda j: jnp.where(j < n_att_blks, j, j + 2 * n_att_blks),
                    n_col_blks=2 * n_att_blks, tm=tm, tn=tn, out_dtype=BF16, name="w_in_q_u")
    newk = _ws_matmul(h, w_in0, row_blk0=0, n_row_blks=NP_TOK // tm,
                      col_map=lambda j: j + n_att_blks, n_col_blks=n_att_blks,
                      tm=tm, tn=tn, out_dtype=F32, name="w_in_k_prompt")
    newv = _ws_matmul(h, w_in0, row_blk0=0, n_row_blks=NP_TOK // tm,
                      col_map=lambda j: j + 2 * n_att_blks, n_col_blks=n_att_blks,
                      tm=tm, tn=tn, out_dtype=F32, name="w_in_v_prompt")
    kv_s = _ws_matmul(h, w_in0, row_blk0=NP_TOK // tm, n_row_blks=NS_TOK // tm,
                      col_map=lambda j: j + n_att_blks, n_col_blks=2 * n_att_blks,
                      tm=tm, tn=tn, out_dtype=BF16, name="w_in_kv_sample")

    att_p = _ctx_attention(qu, newk, newv)
    cls_of_g, patterns = _nbr_classes()
    bias = _rpb_bias(rpb[0].reshape(-1), patterns)
    ck = cache_k[:, 0].reshape(DEC_BATCH * PAST_LEN, D_ATT)
    cv = cache_v[:, 0].reshape(DEC_BATCH * PAST_LEN, D_ATT)
    att_s = _nbr_attention(qu, kv_s, ck, cv, bias, cls_of_g)

    cc, sc = _dft_tables(FGROUP_DIM)
    w1 = jnp.asarray(np.concatenate([cc, -sc], axis=1)).astype(BF16)
    ab = _dft_chan(qu, w1)
    ctp, stp = _dft_tables(SEQ)
    fou_p = _dft_pos_prompt(ab, jnp.asarray(ctp).astype(BF16), jnp.asarray(stp).astype(BF16))
    cts, sts = _dft_tables(DEC_SEQ)
    fou_s = _dft_pos_sample(ab, jnp.asarray(cts).astype(BF16), jnp.asarray(sts).astype(BF16))

    x1p = _wout(att_p, fou_p, w_out[0], xp, mod3, row0=0, name="w_out_prompt")
    x1s = _wout(att_s, fou_s, w_out[0], xs, mod3, row0=NP_TOK, name="w_out_sample")

    h2 = _norm_modulate(x1p, x1s, norm2_g[0][None, :], mod3, shift_part=3, scale_part=4,
                        name="norm2_mod")
    a, wd = _gate_up(h2, w_gate[0], w_up[0], w_down[0])
    fg = final_g[None, :]
    y_prompt = _down(a, wd, x1p, mod3, fg, row0=0,
                     name="ffn_down_prompt").reshape(BATCH, SEQ, D_MODEL)
    y_sample = _down(a, wd, x1s, mod3, fg, row0=NP_TOK,
                     name="ffn_down_sample").reshape(DEC_BATCH, DEC_SEQ, D_MODEL)
    new_cache_k = newk.reshape(BATCH, 1, SEQ, N_HEADS, HEAD_DIM)
    new_cache_v = newv.reshape(BATCH, 1, SEQ, N_HEADS, HEAD_DIM)
    return (y_prompt, y_sample, new_cache_k, new_cache_v)
```

```python
import functools

import numpy as np
import jax
import jax.numpy as jnp
from jax import lax
from jax.experimental import pallas as pl
from jax.experimental.pallas import tpu as pltpu

F32 = jnp.float32
BF16 = jnp.bfloat16

D_MODEL = 4096
BATCH = 32
SEQ = 256
DEC_BATCH = 4
DEC_SEQ = 2048
PAST_LEN = 256
GRID_W = 64
GRID_ROWS = DEC_SEQ // GRID_W
D_ATT = 2048
D_FOURIER = 2048
HEAD_DIM = 128
N_HEADS = 16
N_FGROUPS = 4
FGROUP_DIM = 512
WIN_H = 8
WIN_W = 16
D_FF = 11008
D_FF_PAD = 11264
EPS = 1e-6
NEG_INF = -1e30
SCALE = HEAD_DIM ** -0.5
LOG2E = 1.4426950408889634
SCALE2 = SCALE * LOG2E

NP_TOK = BATCH * SEQ
NS_TOK = DEC_BATCH * DEC_SEQ
N_TOK = NP_TOK + NS_TOK
N_MOD = 8

VMEM_LIMIT = 56 * 1024 * 1024


def _params(n_axes, vmem=VMEM_LIMIT):
    return pltpu.CompilerParams(dimension_semantics=("arbitrary",) * n_axes,
                                vmem_limit_bytes=vmem)


def _mod_row(i, tm):
    nh = NP_TOK // tm
    return jnp.where(i < nh, 0, 1 + (i - nh) // (DEC_SEQ // tm))


def _ada_kernel(c_ref, w_ref, b_ref, o_ref):
    c = c_ref[...]
    s = (c * jax.nn.sigmoid(c)).astype(BF16)
    o_ref[...] = jnp.dot(s, w_ref[...].astype(BF16), preferred_element_type=F32) + b_ref[...]


def _ada(cvec, w_ada, b_ada):
    tn = 512
    n = w_ada.shape[1]
    return pl.pallas_call(
        _ada_kernel,
        grid=(n // tn,),
        in_specs=[pl.BlockSpec((N_MOD, D_MODEL), lambda j: (0, 0)),
                  pl.BlockSpec((D_MODEL, tn), lambda j: (0, j)),
                  pl.BlockSpec((1, tn), lambda j: (0, j))],
        out_specs=pl.BlockSpec((N_MOD, tn), lambda j: (0, j)),
        out_shape=jax.ShapeDtypeStruct((N_MOD, n), F32),
        compiler_params=_params(1),
        name="ada_mod",
    )(cvec, w_ada, b_ada)


ROW_CHUNK = 8


STATS_UNROLL = 8
APPLY_ROWS = 64
APPLY_COLS = 512


def _row_chunks(n_rows, fn):
    def body(c, carry):
        fn(pl.ds(pl.multiple_of(c * ROW_CHUNK, ROW_CHUNK), ROW_CHUNK))
        return carry
    lax.fori_loop(0, n_rows // ROW_CHUNK, body, 0, unroll=STATS_UNROLL)


def _row_col_blocks(n_rows, n_cols, fn):
    def body(c, carry):
        rows = pl.ds(pl.multiple_of(c * APPLY_ROWS, APPLY_ROWS), APPLY_ROWS)
        for j in range(n_cols // APPLY_COLS):
            fn(rows, slice(j * APPLY_COLS, (j + 1) * APPLY_COLS))
        return carry
    lax.fori_loop(0, n_rows // APPLY_ROWS, body, 0)


def _norm_mod(x_ref, g_ref, sh_ref, sc_ref, o_ref, r_ref, gs_ref):
    gs_ref[...] = g_ref[...] * (1.0 + sc_ref[0])

    def stats(rows):
        x = x_ref[rows, :]
        r_ref[rows, :] = lax.rsqrt(jnp.mean(x * x, axis=-1, keepdims=True) + EPS)
    _row_chunks(o_ref.shape[0], stats)

    def apply(rows, cols):
        o_ref[rows, cols] = (x_ref[rows, cols] * r_ref[rows, :] * gs_ref[:, cols]
                             + sh_ref[0, :, cols]).astype(o_ref.dtype)
    _row_col_blocks(o_ref.shape[0], o_ref.shape[1], apply)


def _norm_mod2_kernel(xp_ref, xs_ref, g_ref, sh_ref, sc_ref, o_ref, r_ref, gs_ref, *, nh):
    i = pl.program_id(0)

    @pl.when(i < nh)
    def _():
        _norm_mod(xp_ref, g_ref, sh_ref, sc_ref, o_ref, r_ref, gs_ref)

    @pl.when(i >= nh)
    def _():
        _norm_mod(xs_ref, g_ref, sh_ref, sc_ref, o_ref, r_ref, gs_ref)


def _mod_spec(part, tm):
    return pl.BlockSpec((1, 1, D_MODEL), lambda i: (_mod_row(i, tm) * 6 + part, 0, 0))


def _norm_modulate(xp, xs, g, mod3, *, shift_part, scale_part, name):
    tm = 512
    nh = NP_TOK // tm
    return pl.pallas_call(
        functools.partial(_norm_mod2_kernel, nh=nh),
        grid=(N_TOK // tm,),
        in_specs=[pl.BlockSpec((tm, D_MODEL), lambda i: (jnp.minimum(i, nh - 1), 0)),
                  pl.BlockSpec((tm, D_MODEL), lambda i: (jnp.maximum(i - nh, 0), 0)),
                  pl.BlockSpec((1, D_MODEL), lambda i: (0, 0)),
                  _mod_spec(shift_part, tm), _mod_spec(scale_part, tm)],
        out_specs=pl.BlockSpec((tm, D_MODEL), lambda i: (i, 0)),
        out_shape=jax.ShapeDtypeStruct((N_TOK, D_MODEL), BF16),
        scratch_shapes=[pltpu.VMEM((tm, 1), F32), pltpu.VMEM((1, D_MODEL), F32)],
        compiler_params=_params(1),
        name=name,
    )(xp, xs, g, mod3, mod3)


CAST_ROWS = 512


def _cast_weight(w_ref, wb_ref):
    def body(c, carry):
        r = pl.multiple_of(c * CAST_ROWS, CAST_ROWS)
        wb_ref[pl.ds(r, CAST_ROWS), :] = w_ref[pl.ds(r, CAST_ROWS), :].astype(BF16)
        return carry
    lax.fori_loop(0, w_ref.shape[0] // CAST_ROWS, body, 0)


def _ws_kernel(x_ref, w_ref, *rest):
    o_ref, wb_ref = rest[-2:]

    @pl.when(pl.program_id(1) == 0)
    def _():
        _cast_weight(w_ref, wb_ref)

    o_ref[...] = jnp.dot(x_ref[...], wb_ref[...],
                         preferred_element_type=F32).astype(o_ref.dtype)


def _ws_matmul(x, w, *, row_blk0, n_row_blks, col_map, n_col_blks, tm, tn, out_dtype, name,
               after=None):
    k = x.shape[1]
    operands, in_specs = [x, w], [pl.BlockSpec((tm, k), lambda j, i: (row_blk0 + i, 0)),
                                  pl.BlockSpec((k, tn), lambda j, i: (0, col_map(j)))]
    if after is not None:
        operands.append(after)
        in_specs.append(pl.BlockSpec(memory_space=pl.ANY))
    return pl.pallas_call(
        _ws_kernel,
        grid=(n_col_blks, n_row_blks),
        in_specs=in_specs,
        out_specs=pl.BlockSpec((tm, tn), lambda j, i: (i, j)),
        out_shape=jax.ShapeDtypeStruct((n_row_blks * tm, n_col_blks * tn), out_dtype),
        scratch_shapes=[pltpu.VMEM((k, tn), BF16)],
        compiler_params=_params(2),
        name=name,
    )(*operands)


def _ctx_attn_kernel(q_ref, k_ref, v_ref, o_ref):
    for h in range(N_HEADS):
        hs = slice(h * HEAD_DIM, (h + 1) * HEAD_DIM)
        q = q_ref[:, hs]
        k = k_ref[:, hs].astype(BF16)
        v = v_ref[:, hs].astype(BF16)
        s = lax.dot_general(q, k, (((1,), (1,)), ((), ())), preferred_element_type=F32) * SCALE2
        m = jnp.max(s, axis=-1, keepdims=True)
        p = jnp.exp2(s - m)
        l = jnp.sum(p, axis=-1, keepdims=True)
        o = jnp.dot(p.astype(BF16), v, preferred_element_type=F32) / l
        o_ref[:, hs] = o.astype(o_ref.dtype)


def _ctx_attention(qu, newk, newv):
    return pl.pallas_call(
        _ctx_attn_kernel,
        grid=(BATCH,),
        in_specs=[pl.BlockSpec((SEQ, D_ATT), lambda b: (b, 0)),
                  pl.BlockSpec((SEQ, D_ATT), lambda b: (b, 0)),
                  pl.BlockSpec((SEQ, D_ATT), lambda b: (b, 0))],
        out_specs=pl.BlockSpec((SEQ, D_ATT), lambda b: (b, 0)),
        out_shape=jax.ShapeDtypeStruct((NP_TOK, D_ATT), BF16),
        compiler_params=_params(1),
        name="ctx_attention",
    )(qu, newk, newv)


NA_QROWS = 4
NA_KROWS = NA_QROWS + WIN_H
NA_Q = NA_QROWS * GRID_W
NA_K = NA_KROWS * GRID_W
NA_GROUPS = GRID_ROWS // NA_QROWS


def _nbr_window_start(g):
    lo, hi = 0, GRID_ROWS - NA_KROWS
    if isinstance(g, int):
        return min(max(NA_QROWS * g - WIN_H // 2, lo), hi)
    return jnp.clip(NA_QROWS * g - WIN_H // 2, lo, hi)


def _nbr_classes():
    patterns, cls_of_g = [], []
    for g in range(NA_GROUPS):
        start = _nbr_window_start(g)
        pat = []
        for i in range(NA_QROWS):
            r = NA_QROWS * g + i
            rstart = min(max(r - WIN_H // 2, 0), GRID_ROWS - WIN_H)
            pat.append(tuple((start + j - r + WIN_H - 1) if rstart <= start + j < rstart + WIN_H else None
                             for j in range(NA_KROWS)))
        pat = tuple(pat)
        if pat not in patterns:
            patterns.append(pat)
        cls_of_g.append(patterns.index(pat))
    return tuple(cls_of_g), tuple(patterns)


def _rpb_bias_kernel(rpb_ref, o_ref, t_ref, *, patterns):
    h = pl.program_id(0)
    qc = lax.broadcasted_iota(jnp.int32, (GRID_W, GRID_W), 0)
    kc = lax.broadcasted_iota(jnp.int32, (GRID_W, GRID_W), 1)
    dc = jnp.clip(kc - qc + (WIN_W - 1), 0, 2 * WIN_W - 2)
    cstart = jnp.clip(qc - WIN_W // 2, 0, GRID_W - WIN_W)
    mask = (kc >= cstart) & (kc < cstart + WIN_W)
    n_dc = 2 * WIN_W - 1
    n_dr = 2 * WIN_H - 1
    for dr in range(n_dr):
        t = jnp.zeros((GRID_W, GRID_W), F32)
        for d in range(n_dc):
            t = jnp.where(dc == d, rpb_ref[h * (n_dr * n_dc) + dr * n_dc + d], t)
        t_ref[dr] = jnp.where(mask, t * LOG2E, NEG_INF)
    outside = jnp.full((GRID_W, GRID_W), NEG_INF, F32)
    for c, pat in enumerate(patterns):
        for i in range(NA_QROWS):
            for j in range(NA_KROWS):
                dr = pat[i][j]
                o_ref[c, 0, i * GRID_W:(i + 1) * GRID_W, j * GRID_W:(j + 1) * GRID_W] = (
                    outside if dr is None else t_ref[dr])


def _rpb_bias(rpb_flat, patterns):
    n_cls = len(patterns)
    return pl.pallas_call(
        functools.partial(_rpb_bias_kernel, patterns=patterns),
        grid=(N_HEADS,),
        in_specs=[pl.BlockSpec(memory_space=pltpu.SMEM)],
        out_specs=pl.BlockSpec((n_cls, 1, NA_Q, NA_K), lambda h: (0, h, 0, 0)),
        out_shape=jax.ShapeDtypeStruct((n_cls, N_HEADS, NA_Q, NA_K), F32),
        scratch_shapes=[pltpu.VMEM((2 * WIN_H - 1, GRID_W, GRID_W), F32)],
        compiler_params=_params(1),
        name="rpb_bias",
    )(rpb_flat)


NA_HEADS = 4
NA_COLS = NA_HEADS * HEAD_DIM


def _nbr_attn_kernel(q_ref, k_ref, v_ref, kc_ref, vc_ref, bias_ref, o_ref, kcb_ref, vcb_ref, *,
                     cls_of_g):
    kcb_ref[...] = kc_ref[...].astype(BF16)
    vcb_ref[...] = vc_ref[...].astype(BF16)
    dn = (((1,), (1,)), ((), ()))

    def group_body(g, carry):
        cls = jnp.int32(cls_of_g[0])
        for gg in range(1, NA_GROUPS):
            if cls_of_g[gg] != cls_of_g[gg - 1]:
                cls = jnp.where(g >= gg, cls_of_g[gg], cls)
        q0 = pl.multiple_of(g * NA_Q, NA_Q)
        k0 = pl.multiple_of(_nbr_window_start(g) * GRID_W, GRID_W)
        for h in range(NA_HEADS):
            hs = slice(h * HEAD_DIM, (h + 1) * HEAD_DIM)
            q = q_ref[pl.ds(q0, NA_Q), hs]
            kw = k_ref[pl.ds(k0, NA_K), hs]
            vw = v_ref[pl.ds(k0, NA_K), hs]
            s_loc = lax.dot_general(q, kw, dn, preferred_element_type=F32) * SCALE2 + bias_ref[cls, h]
            s_ctx = lax.dot_general(q, kcb_ref[:, hs], dn, preferred_element_type=F32) * SCALE2
            m = jnp.maximum(jnp.max(s_loc, axis=-1, keepdims=True),
                            jnp.max(s_ctx, axis=-1, keepdims=True))
            p_loc = jnp.exp2(s_loc - m)
            p_ctx = jnp.exp2(s_ctx - m)
            l = jnp.sum(p_loc, axis=-1, keepdims=True) + jnp.sum(p_ctx, axis=-1, keepdims=True)
            o = (jnp.dot(p_loc.astype(BF16), vw, preferred_element_type=F32)
                 + jnp.dot(p_ctx.astype(BF16), vcb_ref[:, hs], preferred_element_type=F32)) / l
            o_ref[pl.ds(q0, NA_Q), hs] = o.astype(o_ref.dtype)
        return carry

    lax.fori_loop(0, NA_GROUPS, group_body, 0)


def _nbr_attention(qu, kv_s, ck, cv, bias, cls_of_g):
    n_hg = N_HEADS // NA_HEADS
    n_cls = bias.shape[0]
    row_blk0 = NP_TOK // DEC_SEQ
    return pl.pallas_call(
        functools.partial(_nbr_attn_kernel, cls_of_g=cls_of_g),
        grid=(n_hg, DEC_BATCH),
        in_specs=[pl.BlockSpec((DEC_SEQ, NA_COLS), lambda g, b: (row_blk0 + b, g)),
                  pl.BlockSpec((DEC_SEQ, NA_COLS), lambda g, b: (b, g)),
                  pl.BlockSpec((DEC_SEQ, NA_COLS), lambda g, b: (b, n_hg + g)),
                  pl.BlockSpec((PAST_LEN, NA_COLS), lambda g, b: (b, g)),
                  pl.BlockSpec((PAST_LEN, NA_COLS), lambda g, b: (b, g)),
                  pl.BlockSpec((n_cls, NA_HEADS, NA_Q, NA_K), lambda g, b: (0, g, 0, 0))],
        out_specs=pl.BlockSpec((DEC_SEQ, NA_COLS), lambda g, b: (b, g)),
        out_shape=jax.ShapeDtypeStruct((NS_TOK, D_ATT), BF16),
        scratch_shapes=[pltpu.VMEM((PAST_LEN, NA_COLS), BF16),
                        pltpu.VMEM((PAST_LEN, NA_COLS), BF16)],
        compiler_params=_params(2),
        name="nbr_attention",
    )(qu, kv_s, kv_s, ck, cv, bias)


def _dft_tables(n):
    idx = np.arange(n, dtype=np.int64)
    ang = (2.0 * np.pi / n) * ((idx[:, None] * idx[None, :]) % n).astype(np.float64)
    return ((np.cos(ang) / np.sqrt(n)).astype(np.float32),
            (-np.sin(ang) / np.sqrt(n)).astype(np.float32))


def _dft_chan_kernel(u_ref, w_ref, o_ref):
    for g in range(N_FGROUPS):
        u = u_ref[:, g * FGROUP_DIM:(g + 1) * FGROUP_DIM]
        o_ref[:, g * 2 * FGROUP_DIM:(g + 1) * 2 * FGROUP_DIM] = jnp.dot(
            u, w_ref[...], preferred_element_type=F32).astype(o_ref.dtype)


def _dft_chan(qu, w1):
    tm = 1024
    return pl.pallas_call(
        _dft_chan_kernel,
        grid=(N_TOK // tm,),
        in_specs=[pl.BlockSpec((tm, D_FOURIER), lambda i: (i, 1)),
                  pl.BlockSpec((FGROUP_DIM, 2 * FGROUP_DIM), lambda i: (0, 0))],
        out_specs=pl.BlockSpec((tm, 2 * D_FOURIER), lambda i: (i, 0)),
        out_shape=jax.ShapeDtypeStruct((N_TOK, 2 * D_FOURIER), BF16),
        compiler_params=_params(1),
        name="dft_channels",
    )(qu, w1)


def _dft_pos_kernel(ct_ref, st_ref, ab_ref, o_ref):
    n_in = ct_ref.shape[1]
    n_out = ct_ref.shape[0]
    for s in range(ab_ref.shape[0] // n_in):
        for g in range(ab_ref.shape[1] // (2 * FGROUP_DIM)):
            rows = slice(s * n_in, (s + 1) * n_in)
            a = ab_ref[rows, g * 2 * FGROUP_DIM:g * 2 * FGROUP_DIM + FGROUP_DIM]
            b = ab_ref[rows, g * 2 * FGROUP_DIM + FGROUP_DIM:(g + 1) * 2 * FGROUP_DIM]
            o = (jnp.dot(ct_ref[...], a, preferred_element_type=F32)
                 + jnp.dot(st_ref[...], b, preferred_element_type=F32))
            o_ref[s * n_out:(s + 1) * n_out, g * FGROUP_DIM:(g + 1) * FGROUP_DIM] = o.astype(o_ref.dtype)


def _dft_pos_prompt(ab, ct, st):
    n_seq = 4
    rows = n_seq * SEQ
    return pl.pallas_call(
        _dft_pos_kernel,
        grid=(BATCH // n_seq,),
        in_specs=[pl.BlockSpec((SEQ, SEQ), lambda b: (0, 0)),
                  pl.BlockSpec((SEQ, SEQ), lambda b: (0, 0)),
                  pl.BlockSpec((rows, 2 * D_FOURIER), lambda b: (b, 0))],
        out_specs=pl.BlockSpec((rows, D_FOURIER), lambda b: (b, 0)),
        out_shape=jax.ShapeDtypeStruct((NP_TOK, D_FOURIER), BF16),
        compiler_params=_params(1),
        name="dft_pos_prompt",
    )(ct, st, ab)


def _dft_pos_sample(ab, ct, st):
    tr = 1024
    row_blk0 = NP_TOK // DEC_SEQ
    return pl.pallas_call(
        _dft_pos_kernel,
        grid=(DEC_BATCH, N_FGROUPS, DEC_SEQ // tr),
        in_specs=[pl.BlockSpec((tr, DEC_SEQ), lambda b, g, t: (t, 0)),
                  pl.BlockSpec((tr, DEC_SEQ), lambda b, g, t: (t, 0)),
                  pl.BlockSpec((DEC_SEQ, 2 * FGROUP_DIM), lambda b, g, t: (row_blk0 + b, g))],
        out_specs=pl.BlockSpec((tr, FGROUP_DIM), lambda b, g, t: (b * (DEC_SEQ // tr) + t, g)),
        out_shape=jax.ShapeDtypeStruct((NS_TOK, D_FOURIER), BF16),
        compiler_params=_params(3),
        name="dft_pos_sample",
    )(ct, st, ab)


def _wout_kernel(a_ref, f_ref, w_ref, x_ref, g_ref, o_ref, wb_ref):
    @pl.when(pl.program_id(1) == 0)
    def _():
        _cast_weight(w_ref, wb_ref)

    acc = (jnp.dot(a_ref[...], wb_ref[0:D_ATT, :], preferred_element_type=F32)
           + jnp.dot(f_ref[...], wb_ref[D_ATT:D_ATT + D_FOURIER, :], preferred_element_type=F32))
    o_ref[...] = x_ref[...] + g_ref[0] * acc


def _wout(att, fou, w_out, x, mod3, *, row0, name):
    tm, tn = 1024, 512
    n_rows = x.shape[0]
    blk0 = row0 // tm
    return pl.pallas_call(
        _wout_kernel,
        grid=(D_MODEL // tn, n_rows // tm),
        in_specs=[pl.BlockSpec((tm, D_ATT), lambda j, i: (i, 0)),
                  pl.BlockSpec((tm, D_FOURIER), lambda j, i: (i, 0)),
                  pl.BlockSpec((D_MODEL, tn), lambda j, i: (0, j)),
                  pl.BlockSpec((tm, tn), lambda j, i: (i, j)),
                  pl.BlockSpec((1, 1, tn), lambda j, i: (_mod_row(blk0 + i, tm) * 6 + 2, 0, j))],
        out_specs=pl.BlockSpec((tm, tn), lambda j, i: (i, j)),
        out_shape=jax.ShapeDtypeStruct((n_rows, D_MODEL), F32),
        scratch_shapes=[pltpu.VMEM((D_MODEL, tn), BF16)],
        compiler_params=_params(2),
        name=name,
    )(att, fou, w_out, x, mod3)


FF_TN = 256


def _gate_up_kernel(h_ref, wg_ref, wu_ref, wd_ref, o_ref, wdb_ref, wgb_ref, wub_ref, *, n_real):
    j = pl.program_id(0)

    @pl.when((pl.program_id(1) == 0) & (j < n_real))
    def _():
        _cast_weight(wg_ref, wgb_ref)
        _cast_weight(wu_ref, wub_ref)

    @pl.when(j < n_real)
    def _():
        for c in range(h_ref.shape[0] // FF_ROWS):
            rs = slice(c * FF_ROWS, (c + 1) * FF_ROWS)
            h = h_ref[rs, :]
            g = jnp.dot(h, wgb_ref[...], preferred_element_type=F32)
            u = jnp.dot(h, wub_ref[...], preferred_element_type=F32)
            o_ref[rs, :] = (g * jax.nn.sigmoid(g) * u).astype(o_ref.dtype)
        wdb_ref[...] = wd_ref[...].astype(wdb_ref.dtype)

    @pl.when(j >= n_real)
    def _():
        o_ref[...] = jnp.zeros_like(o_ref)
        wdb_ref[...] = jnp.zeros_like(wdb_ref)


FF_ROWS = 512


def _gate_up(h2, w_gate, w_up, w_down):
    tm, tn = 2048, FF_TN
    n_real = D_FF // tn
    ni = N_TOK // tm
    wd_rows = D_FF // (n_real * ni)
    assert wd_rows * n_real * ni == D_FF and wd_rows % 16 == 0
    assert (D_FF_PAD - D_FF) == (D_FF_PAD // tn - n_real) * ni * wd_rows
    wmap = lambda j, i: (0, jnp.minimum(j, n_real - 1))
    return pl.pallas_call(
        functools.partial(_gate_up_kernel, n_real=n_real),
        grid=(D_FF_PAD // tn, ni),
        in_specs=[pl.BlockSpec((tm, D_MODEL), lambda j, i: (i, 0)),
                  pl.BlockSpec((D_MODEL, tn), wmap),
                  pl.BlockSpec((D_MODEL, tn), wmap),
                  pl.BlockSpec((wd_rows, D_MODEL),
                               lambda j, i: (jnp.minimum(j * ni + i, n_real * ni - 1), 0))],
        out_specs=[pl.BlockSpec((tm, tn), lambda j, i: (i, j)),
                   pl.BlockSpec((wd_rows, D_MODEL), lambda j, i: (j * ni + i, 0))],
        out_shape=[jax.ShapeDtypeStruct((N_TOK, D_FF_PAD), BF16),
                   jax.ShapeDtypeStruct((D_FF_PAD, D_MODEL), BF16)],
        scratch_shapes=[pltpu.VMEM((D_MODEL, tn), BF16), pltpu.VMEM((D_MODEL, tn), BF16)],
        compiler_params=_params(2),
        name="ffn_gate_up",
    )(h2, w_gate, w_up, w_down)


def _down_kernel(a_ref, w_ref, x_ref, g_ref, fg_ref, o_ref, r_ref, *, nk):
    k = pl.program_id(1)

    def accumulate(first):
        for n in range(D_MODEL // DOWN_TN):
            ns = slice(n * DOWN_TN, (n + 1) * DOWN_TN)
            part = jnp.dot(a_ref[...], w_ref[:, ns], preferred_element_type=F32)
            if first:
                o_ref[:, ns] = part
            else:
                o_ref[:, ns] += part

    @pl.when(k == 0)
    def _():
        accumulate(True)

    @pl.when(k > 0)
    def _():
        accumulate(False)

    @pl.when(k == nk - 1)
    def _():
        lanes = 128

        def residual_stats(c, carry):
            rows = pl.ds(pl.multiple_of(c * APPLY_ROWS, APPLY_ROWS), APPLY_ROWS)
            ss = jnp.zeros((APPLY_ROWS, lanes), F32)
            for j in range(D_MODEL // APPLY_COLS):
                cols = slice(j * APPLY_COLS, (j + 1) * APPLY_COLS)
                x2 = x_ref[rows, cols] + g_ref[0, :, cols] * o_ref[rows, cols]
                o_ref[rows, cols] = x2
                sq = x2 * x2
                for q in range(APPLY_COLS // lanes):
                    ss = ss + sq[:, q * lanes:(q + 1) * lanes]
            ms = jnp.sum(ss, axis=-1, keepdims=True) * (1.0 / D_MODEL)
            r_ref[rows, :] = lax.rsqrt(ms + EPS)
            return carry
        lax.fori_loop(0, o_ref.shape[0] // APPLY_ROWS, residual_stats, 0)

        def normalise(rows, cols):
            o_ref[rows, cols] = o_ref[rows, cols] * r_ref[rows, :] * fg_ref[:, cols]
        _row_col_blocks(o_ref.shape[0], o_ref.shape[1], normalise)


DOWN_TN = 512


def _down(a, wd, x1, mod3, final_g, *, row0, name):
    tm, tk = 512, 1024
    nk = D_FF_PAD // tk
    n_rows = x1.shape[0]
    blk0 = row0 // tm
    return pl.pallas_call(
        functools.partial(_down_kernel, nk=nk),
        grid=(n_rows // tm, nk),
        in_specs=[pl.BlockSpec((tm, tk), lambda i, k: (blk0 + i, k)),
                  pl.BlockSpec((tk, D_MODEL), lambda i, k: (k, 0)),
                  pl.BlockSpec((tm, D_MODEL), lambda i, k: (i, 0)),
                  pl.BlockSpec((1, 1, D_MODEL), lambda i, k: (_mod_row(blk0 + i, tm) * 6 + 5, 0, 0)),
                  pl.BlockSpec((1, D_MODEL), lambda i, k: (0, 0))],
        out_specs=pl.BlockSpec((tm, D_MODEL), lambda i, k: (i, 0)),
        out_shape=jax.ShapeDtypeStruct((n_rows, D_MODEL), F32),
        scratch_shapes=[pltpu.VMEM((tm, 1), F32)],
        compiler_params=_params(2),
        name=name,
    )(a, wd, x1, mod3, final_g)


def kernel(x_prompt, x_sample, cache_k, cache_v, c, c_ctx, w_ada, b_ada, norm1_g, w_in, rpb,
           w_out, norm2_g, w_gate, w_up, w_down, final_g):
    xp = x_prompt.reshape(NP_TOK, D_MODEL)
    xs = x_sample.reshape(NS_TOK, D_MODEL)

    cvec = jnp.concatenate([c_ctx[None, :], c, jnp.zeros((N_MOD - 1 - DEC_BATCH, D_MODEL), F32)], axis=0)
    mod = _ada(cvec, w_ada[0], b_ada[0][None, :])
    mod3 = mod.reshape(N_MOD * 6, 1, D_MODEL)

    h = _norm_modulate(xp, xs, norm1_g[0][None, :], mod3, shift_part=0, scale_part=1,
                       name="norm1_mod")

    w_in0 = w_in[0]
    tm, tn = 1024, 512
    n_att_blks = D_ATT // tn
    qu = _ws_matmul(h, w_in0, row_blk0=0, n_row_blks=N_TOK // tm,
                    col_map=lambda j: jnp.where(j < n_att_blks, j, j + 2 * n_att_blks),
                    n_col_blks=2 * n_att_blks, tm=tm, tn=tn, out_dtype=BF16, name="w_in_q_u")
    kv_s = _ws_matmul(h, w_in0, row_blk0=NP_TOK // tm, n_row_blks=NS_TOK // tm,
                      col_map=lambda j: j + n_att_blks, n_col_blks=2 * n_att_blks,
                      tm=tm, tn=tn, out_dtype=BF16, name="w_in_kv_sample")

    cls_of_g, patterns = _nbr_classes()
    bias = _rpb_bias(rpb[0].reshape(-1), patterns)
    ck = cache_k[:, 0].reshape(DEC_BATCH * PAST_LEN, D_ATT)
    cv = cache_v[:, 0].reshape(DEC_BATCH * PAST_LEN, D_ATT)
    newk = _ws_matmul(h, w_in0, row_blk0=0, n_row_blks=NP_TOK // tm,
                      col_map=lambda j: j + n_att_blks, n_col_blks=n_att_blks,
                      tm=tm, tn=tn, out_dtype=F32, name="w_in_k_prompt", after=kv_s)
    att_s = _nbr_attention(qu, kv_s, ck, cv, bias, cls_of_g)
    newv = _ws_matmul(h, w_in0, row_blk0=0, n_row_blks=NP_TOK // tm,
                      col_map=lambda j: j + 2 * n_att_blks, n_col_blks=n_att_blks,
                      tm=tm, tn=tn, out_dtype=F32, name="w_in_v_prompt", after=att_s)
    att_p = _ctx_attention(qu, newk, newv)

    cc, sc = _dft_tables(FGROUP_DIM)
    w1 = jnp.asarray(np.concatenate([cc, -sc], axis=1)).astype(BF16)
    ab = _dft_chan(qu, w1)
    ctp, stp = _dft_tables(SEQ)
    fou_p = _dft_pos_prompt(ab, jnp.asarray(ctp).astype(BF16), jnp.asarray(stp).astype(BF16))
    cts, sts = _dft_tables(DEC_SEQ)
    fou_s = _dft_pos_sample(ab, jnp.asarray(cts).astype(BF16), jnp.asarray(sts).astype(BF16))

    x1p = _wout(att_p, fou_p, w_out[0], xp, mod3, row0=0, name="w_out_prompt")
    x1s = _wout(att_s, fou_s, w_out[0], xs, mod3, row0=NP_TOK, name="w_out_sample")

    h2 = _norm_modulate(x1p, x1s, norm2_g[0][None, :], mod3, shift_part=3, scale_part=4,
                        name="norm2_mod")
    a, wd = _gate_up(h2, w_gate[0], w_up[0], w_down[0])
    fg = final_g[None, :]
    y_prompt = _down(a, wd, x1p, mod3, fg, row0=0,
                     name="ffn_down_prompt").reshape(BATCH, SEQ, D_MODEL)
    y_sample = _down(a, wd, x1s, mod3, fg, row0=NP_TOK,
                     name="ffn_down_sample").reshape(DEC_BATCH, DEC_SEQ, D_MODEL)
    new_cache_k = newk.reshape(BATCH, 1, SEQ, N_HEADS, HEAD_DIM)
    new_cache_v = newv.reshape(BATCH, 1, SEQ, N_HEADS, HEAD_DIM)
    return (y_prompt, y_sample, new_cache_k, new_cache_v)
```

```python
import functools

import numpy as np
import jax
import jax.numpy as jnp
from jax import lax
from jax.experimental import pallas as pl
from jax.experimental.pallas import tpu as pltpu

F32 = jnp.float32
BF16 = jnp.bfloat16

D_MODEL = 4096
BATCH = 32
SEQ = 256
DEC_BATCH = 4
DEC_SEQ = 2048
PAST_LEN = 256
GRID_W = 64
GRID_ROWS = DEC_SEQ // GRID_W
D_ATT = 2048
D_FOURIER = 2048
HEAD_DIM = 128
N_HEADS = 16
N_FGROUPS = 4
FGROUP_DIM = 512
WIN_H = 8
WIN_W = 16
D_FF = 11008
D_FF_PAD = 11264
EPS = 1e-6
NEG_INF = -1e30
SCALE = HEAD_DIM ** -0.5
LOG2E = 1.4426950408889634
SCALE2 = SCALE * LOG2E

NP_TOK = BATCH * SEQ
NS_TOK = DEC_BATCH * DEC_SEQ
N_TOK = NP_TOK + NS_TOK
N_MOD = 8

VMEM_LIMIT = 56 * 1024 * 1024
VMEM_LIMIT_BIG = 60 * 1024 * 1024


def _params(n_axes, vmem=VMEM_LIMIT):
    return pltpu.CompilerParams(dimension_semantics=("arbitrary",) * n_axes,
                                vmem_limit_bytes=vmem)


def _mod_row(i, tm):
    nh = NP_TOK // tm
    return jnp.where(i < nh, 0, 1 + (i - nh) // (DEC_SEQ // tm))


def _ada_kernel(c_ref, w_ref, b_ref, o_ref):
    c = c_ref[...]
    s = (c * jax.nn.sigmoid(c)).astype(BF16)
    o_ref[...] = jnp.dot(s, w_ref[...].astype(BF16), preferred_element_type=F32) + b_ref[...]


def _ada(cvec, w_ada, b_ada):
    tn = 512
    n = w_ada.shape[1]
    return pl.pallas_call(
        _ada_kernel,
        grid=(n // tn,),
        in_specs=[pl.BlockSpec((N_MOD, D_MODEL), lambda j: (0, 0)),
                  pl.BlockSpec((D_MODEL, tn), lambda j: (0, j)),
                  pl.BlockSpec((1, tn), lambda j: (0, j))],
        out_specs=pl.BlockSpec((N_MOD, tn), lambda j: (0, j)),
        out_shape=jax.ShapeDtypeStruct((N_MOD, n), F32),
        compiler_params=_params(1),
        name="ada_mod",
    )(cvec, w_ada, b_ada)


ROW_CHUNK = 8


STATS_UNROLL = 8
APPLY_ROWS = 64
APPLY_COLS = 512


def _row_chunks(n_rows, fn):
    def body(c, carry):
        fn(pl.ds(pl.multiple_of(c * ROW_CHUNK, ROW_CHUNK), ROW_CHUNK))
        return carry
    lax.fori_loop(0, n_rows // ROW_CHUNK, body, 0, unroll=STATS_UNROLL)


def _row_col_blocks(n_rows, n_cols, fn):
    def body(c, carry):
        rows = pl.ds(pl.multiple_of(c * APPLY_ROWS, APPLY_ROWS), APPLY_ROWS)
        for j in range(n_cols // APPLY_COLS):
            fn(rows, slice(j * APPLY_COLS, (j + 1) * APPLY_COLS))
        return carry
    lax.fori_loop(0, n_rows // APPLY_ROWS, body, 0)


def _norm_mod(x_ref, g_ref, sh_ref, sc_ref, o_ref, r_ref, gs_ref):
    gs_ref[...] = g_ref[...] * (1.0 + sc_ref[0])

    def stats(rows):
        x = x_ref[rows, :]
        r_ref[rows, :] = lax.rsqrt(jnp.mean(x * x, axis=-1, keepdims=True) + EPS)
    _row_chunks(o_ref.shape[0], stats)

    def apply(rows, cols):
        o_ref[rows, cols] = (x_ref[rows, cols] * r_ref[rows, :] * gs_ref[:, cols]
                             + sh_ref[0, :, cols]).astype(o_ref.dtype)
    _row_col_blocks(o_ref.shape[0], o_ref.shape[1], apply)


def _norm_mod2_kernel(xp_ref, xs_ref, g_ref, sh_ref, sc_ref, o_ref, r_ref, gs_ref, *, nh):
    i = pl.program_id(0)

    @pl.when(i < nh)
    def _():
        _norm_mod(xp_ref, g_ref, sh_ref, sc_ref, o_ref, r_ref, gs_ref)

    @pl.when(i >= nh)
    def _():
        _norm_mod(xs_ref, g_ref, sh_ref, sc_ref, o_ref, r_ref, gs_ref)


def _mod_spec(part, tm):
    return pl.BlockSpec((1, 1, D_MODEL), lambda i: (_mod_row(i, tm) * 6 + part, 0, 0))


def _norm_modulate(xp, xs, g, mod3, *, shift_part, scale_part, name):
    tm = 512
    nh = NP_TOK // tm
    return pl.pallas_call(
        functools.partial(_norm_mod2_kernel, nh=nh),
        grid=(N_TOK // tm,),
        in_specs=[pl.BlockSpec((tm, D_MODEL), lambda i: (jnp.minimum(i, nh - 1), 0)),
                  pl.BlockSpec((tm, D_MODEL), lambda i: (jnp.maximum(i - nh, 0), 0)),
                  pl.BlockSpec((1, D_MODEL), lambda i: (0, 0)),
                  _mod_spec(shift_part, tm), _mod_spec(scale_part, tm)],
        out_specs=pl.BlockSpec((tm, D_MODEL), lambda i: (i, 0)),
        out_shape=jax.ShapeDtypeStruct((N_TOK, D_MODEL), BF16),
        scratch_shapes=[pltpu.VMEM((tm, 1), F32), pltpu.VMEM((1, D_MODEL), F32)],
        compiler_params=_params(1),
        name=name,
    )(xp, xs, g, mod3, mod3)


CAST_ROWS = 512


def _cast_weight(w_ref, wb_ref):
    def body(c, carry):
        r = pl.multiple_of(c * CAST_ROWS, CAST_ROWS)
        wb_ref[pl.ds(r, CAST_ROWS), :] = w_ref[pl.ds(r, CAST_ROWS), :].astype(BF16)
        return carry
    lax.fori_loop(0, w_ref.shape[0] // CAST_ROWS, body, 0)


WS_ROWS = 512


def _ws_kernel(x_ref, w_ref, o_ref, wb_ref):
    @pl.when(pl.program_id(1) == 0)
    def _():
        _cast_weight(w_ref, wb_ref)

    for c in range(x_ref.shape[0] // WS_ROWS):
        rs = slice(c * WS_ROWS, (c + 1) * WS_ROWS)
        res = jnp.dot(x_ref[rs, :], wb_ref[...], preferred_element_type=F32).astype(o_ref.dtype)
        if len(o_ref.shape) == 2:
            o_ref[rs, :] = res
        else:
            ts = slice(c * WS_ROWS // SUBLANES, (c + 1) * WS_ROWS // SUBLANES)
            for t in range(o_ref.shape[1]):
                o_ref[ts, t] = res[:, t * LANES:(t + 1) * LANES].reshape(WS_ROWS // SUBLANES, SUBLANES, LANES)


SUBLANES, LANES = 8, 128


def _ws_matmul(x, w, *, row_blk0, n_row_blks, col_map, n_col_blks, tm, tn, out_dtype, name,
               vmem=VMEM_LIMIT, tiled_out=False):
    k = x.shape[1]
    if tiled_out:
        out_spec = pl.BlockSpec((tm // SUBLANES, tn // LANES, SUBLANES, LANES), lambda j, i: (i, j, 0, 0))
        out_shape = (n_row_blks * tm // SUBLANES, n_col_blks * tn // LANES, SUBLANES, LANES)
    else:
        out_spec = pl.BlockSpec((tm, tn), lambda j, i: (i, j))
        out_shape = (n_row_blks * tm, n_col_blks * tn)
    return pl.pallas_call(
        _ws_kernel,
        grid=(n_col_blks, n_row_blks),
        in_specs=[pl.BlockSpec((tm, k), lambda j, i: (row_blk0 + i, 0)),
                  pl.BlockSpec((k, tn), lambda j, i: (0, col_map(j)))],
        out_specs=out_spec,
        out_shape=jax.ShapeDtypeStruct(out_shape, out_dtype),
        scratch_shapes=[pltpu.VMEM((k, tn), BF16)],
        compiler_params=_params(2, vmem),
        name=name,
    )(x, w)


def _ctx_attn_kernel(q_ref, k_ref, v_ref, o_ref):
    for h in range(N_HEADS):
        hs = slice(h * HEAD_DIM, (h + 1) * HEAD_DIM)
        q = q_ref[:, hs]
        k = k_ref[:, h].reshape(SEQ, HEAD_DIM).astype(BF16)
        v = v_ref[:, h].reshape(SEQ, HEAD_DIM).astype(BF16)
        s = lax.dot_general(q, k, (((1,), (1,)), ((), ())), preferred_element_type=F32) * SCALE2
        m = jnp.max(s, axis=-1, keepdims=True)
        p = jnp.exp2(s - m)
        l = jnp.sum(p, axis=-1, keepdims=True)
        o = jnp.dot(p.astype(BF16), v, preferred_element_type=F32) / l
        o_ref[:, hs] = o.astype(o_ref.dtype)


def _ctx_attention(qu, newk, newv):
    return pl.pallas_call(
        _ctx_attn_kernel,
        grid=(BATCH,),
        in_specs=[pl.BlockSpec((SEQ, D_ATT), lambda b: (b, 0)),
                  pl.BlockSpec((SEQ // SUBLANES, N_HEADS, SUBLANES, LANES), lambda b: (b, 0, 0, 0)),
                  pl.BlockSpec((SEQ // SUBLANES, N_HEADS, SUBLANES, LANES), lambda b: (b, 0, 0, 0))],
        out_specs=pl.BlockSpec((SEQ, D_ATT), lambda b: (b, 0)),
        out_shape=jax.ShapeDtypeStruct((NP_TOK, D_ATT), BF16),
        compiler_params=_params(1),
        name="ctx_attention",
    )(qu, newk, newv)


NA_QROWS = 4
NA_KROWS = NA_QROWS + WIN_H
NA_Q = NA_QROWS * GRID_W
NA_K = NA_KROWS * GRID_W
NA_GROUPS = GRID_ROWS // NA_QROWS


def _nbr_window_start(g):
    lo, hi = 0, GRID_ROWS - NA_KROWS
    if isinstance(g, int):
        return min(max(NA_QROWS * g - WIN_H // 2, lo), hi)
    return jnp.clip(NA_QROWS * g - WIN_H // 2, lo, hi)


def _nbr_classes():
    patterns, cls_of_g = [], []
    for g in range(NA_GROUPS):
        start = _nbr_window_start(g)
        pat = []
        for i in range(NA_QROWS):
            r = NA_QROWS * g + i
            rstart = min(max(r - WIN_H // 2, 0), GRID_ROWS - WIN_H)
            pat.append(tuple((start + j - r + WIN_H - 1) if rstart <= start + j < rstart + WIN_H else None
                             for j in range(NA_KROWS)))
        pat = tuple(pat)
        if pat not in patterns:
            patterns.append(pat)
        cls_of_g.append(patterns.index(pat))
    return tuple(cls_of_g), tuple(patterns)


def _rpb_bias_kernel(rpb_ref, o_ref, t_ref, *, patterns):
    h = pl.program_id(0)
    qc = lax.broadcasted_iota(jnp.int32, (GRID_W, GRID_W), 0)
    kc = lax.broadcasted_iota(jnp.int32, (GRID_W, GRID_W), 1)
    dc = jnp.clip(kc - qc + (WIN_W - 1), 0, 2 * WIN_W - 2)
    cstart = jnp.clip(qc - WIN_W // 2, 0, GRID_W - WIN_W)
    mask = (kc >= cstart) & (kc < cstart + WIN_W)
    n_dc = 2 * WIN_W - 1
    n_dr = 2 * WIN_H - 1
    for dr in range(n_dr):
        t = jnp.zeros((GRID_W, GRID_W), F32)
        for d in range(n_dc):
            t = jnp.where(dc == d, rpb_ref[h * (n_dr * n_dc) + dr * n_dc + d], t)
        t_ref[dr] = jnp.where(mask, t * LOG2E, NEG_INF)
    outside = jnp.full((GRID_W, GRID_W), NEG_INF, F32)
    for c, pat in enumerate(patterns):
        for i in range(NA_QROWS):
            for j in range(NA_KROWS):
                dr = pat[i][j]
                o_ref[c, 0, i * GRID_W:(i + 1) * GRID_W, j * GRID_W:(j + 1) * GRID_W] = (
                    outside if dr is None else t_ref[dr])


def _rpb_bias(rpb_flat, patterns):
    n_cls = len(patterns)
    return pl.pallas_call(
        functools.partial(_rpb_bias_kernel, patterns=patterns),
        grid=(N_HEADS,),
        in_specs=[pl.BlockSpec(memory_space=pltpu.SMEM)],
        out_specs=pl.BlockSpec((n_cls, 1, NA_Q, NA_K), lambda h: (0, h, 0, 0)),
        out_shape=jax.ShapeDtypeStruct((n_cls, N_HEADS, NA_Q, NA_K), F32),
        scratch_shapes=[pltpu.VMEM((2 * WIN_H - 1, GRID_W, GRID_W), F32)],
        compiler_params=_params(1),
        name="rpb_bias",
    )(rpb_flat)


NA_HEADS = 4
NA_COLS = NA_HEADS * HEAD_DIM


def _nbr_attn_kernel(q_ref, k_ref, v_ref, kc_ref, vc_ref, bias_ref, o_ref, kcb_ref, vcb_ref, *,
                     cls_of_g):
    kcb_ref[...] = kc_ref[...].astype(BF16)
    vcb_ref[...] = vc_ref[...].astype(BF16)
    dn = (((1,), (1,)), ((), ()))

    def group_body(g, carry):
        cls = jnp.int32(cls_of_g[0])
        for gg in range(1, NA_GROUPS):
            if cls_of_g[gg] != cls_of_g[gg - 1]:
                cls = jnp.where(g >= gg, cls_of_g[gg], cls)
        q0 = pl.multiple_of(g * NA_Q, NA_Q)
        k0 = pl.multiple_of(_nbr_window_start(g) * GRID_W, GRID_W)
        for h in range(NA_HEADS):
            hs = slice(h * HEAD_DIM, (h + 1) * HEAD_DIM)
            q = q_ref[pl.ds(q0, NA_Q), hs]
            kw = k_ref[pl.ds(k0, NA_K), hs]
            vw = v_ref[pl.ds(k0, NA_K), hs]
            s_loc = lax.dot_general(q, kw, dn, preferred_element_type=F32) * SCALE2 + bias_ref[cls, h]
            s_ctx = lax.dot_general(q, kcb_ref[:, hs], dn, preferred_element_type=F32) * SCALE2
            m = jnp.maximum(jnp.max(s_loc, axis=-1, keepdims=True),
                            jnp.max(s_ctx, axis=-1, keepdims=True))
            p_loc = jnp.exp2(s_loc - m)
            p_ctx = jnp.exp2(s_ctx - m)
            l = jnp.sum(p_loc, axis=-1, keepdims=True) + jnp.sum(p_ctx, axis=-1, keepdims=True)
            o = (jnp.dot(p_loc.astype(BF16), vw, preferred_element_type=F32)
                 + jnp.dot(p_ctx.astype(BF16), vcb_ref[:, hs], preferred_element_type=F32)) / l
            o_ref[pl.ds(q0, NA_Q), hs] = o.astype(o_ref.dtype)
        return carry

    lax.fori_loop(0, NA_GROUPS, group_body, 0)


def _nbr_attention(qu, kv_s, ck, cv, bias, cls_of_g):
    n_hg = N_HEADS // NA_HEADS
    n_cls = bias.shape[0]
    row_blk0 = NP_TOK // DEC_SEQ
    return pl.pallas_call(
        functools.partial(_nbr_attn_kernel, cls_of_g=cls_of_g),
        grid=(n_hg, DEC_BATCH),
        in_specs=[pl.BlockSpec((DEC_SEQ, NA_COLS), lambda g, b: (row_blk0 + b, g)),
                  pl.BlockSpec((DEC_SEQ, NA_COLS), lambda g, b: (b, g)),
                  pl.BlockSpec((DEC_SEQ, NA_COLS), lambda g, b: (b, n_hg + g)),
                  pl.BlockSpec((PAST_LEN, NA_COLS), lambda g, b: (b, g)),
                  pl.BlockSpec((PAST_LEN, NA_COLS), lambda g, b: (b, g)),
                  pl.BlockSpec((n_cls, NA_HEADS, NA_Q, NA_K), lambda g, b: (0, g, 0, 0))],
        out_specs=pl.BlockSpec((DEC_SEQ, NA_COLS), lambda g, b: (b, g)),
        out_shape=jax.ShapeDtypeStruct((NS_TOK, D_ATT), BF16),
        scratch_shapes=[pltpu.VMEM((PAST_LEN, NA_COLS), BF16),
                        pltpu.VMEM((PAST_LEN, NA_COLS), BF16)],
        compiler_params=_params(2),
        name="nbr_attention",
    )(qu, kv_s, kv_s, ck, cv, bias)


def _dft_tables(n):
    idx = np.arange(n, dtype=np.int64)
    ang = (2.0 * np.pi / n) * ((idx[:, None] * idx[None, :]) % n).astype(np.float64)
    return ((np.cos(ang) / np.sqrt(n)).astype(np.float32),
            (-np.sin(ang) / np.sqrt(n)).astype(np.float32))


def _dft_chan_kernel(u_ref, w_ref, o_ref):
    for g in range(N_FGROUPS):
        u = u_ref[:, g * FGROUP_DIM:(g + 1) * FGROUP_DIM]
        o_ref[:, g * 2 * FGROUP_DIM:(g + 1) * 2 * FGROUP_DIM] = jnp.dot(
            u, w_ref[...], preferred_element_type=F32).astype(o_ref.dtype)


def _dft_chan(qu, w1):
    tm = 1024
    return pl.pallas_call(
        _dft_chan_kernel,
        grid=(N_TOK // tm,),
        in_specs=[pl.BlockSpec((tm, D_FOURIER), lambda i: (i, 1)),
                  pl.BlockSpec((FGROUP_DIM, 2 * FGROUP_DIM), lambda i: (0, 0))],
        out_specs=pl.BlockSpec((tm, 2 * D_FOURIER), lambda i: (i, 0)),
        out_shape=jax.ShapeDtypeStruct((N_TOK, 2 * D_FOURIER), BF16),
        compiler_params=_params(1),
        name="dft_channels",
    )(qu, w1)


def _dft_pos_kernel(ct_ref, st_ref, ab_ref, o_ref):
    n_in = ct_ref.shape[1]
    n_out = ct_ref.shape[0]
    for s in range(ab_ref.shape[0] // n_in):
        for g in range(ab_ref.shape[1] // (2 * FGROUP_DIM)):
            rows = slice(s * n_in, (s + 1) * n_in)
            a = ab_ref[rows, g * 2 * FGROUP_DIM:g * 2 * FGROUP_DIM + FGROUP_DIM]
            b = ab_ref[rows, g * 2 * FGROUP_DIM + FGROUP_DIM:(g + 1) * 2 * FGROUP_DIM]
            o = (jnp.dot(ct_ref[...], a, preferred_element_type=F32)
                 + jnp.dot(st_ref[...], b, preferred_element_type=F32))
            o_ref[s * n_out:(s + 1) * n_out, g * FGROUP_DIM:(g + 1) * FGROUP_DIM] = o.astype(o_ref.dtype)


def _dft_pos_prompt(ab, ct, st):
    n_seq = 4
    rows = n_seq * SEQ
    return pl.pallas_call(
        _dft_pos_kernel,
        grid=(BATCH // n_seq,),
        in_specs=[pl.BlockSpec((SEQ, SEQ), lambda b: (0, 0)),
                  pl.BlockSpec((SEQ, SEQ), lambda b: (0, 0)),
                  pl.BlockSpec((rows, 2 * D_FOURIER), lambda b: (b, 0))],
        out_specs=pl.BlockSpec((rows, D_FOURIER), lambda b: (b, 0)),
        out_shape=jax.ShapeDtypeStruct((NP_TOK, D_FOURIER), BF16),
        compiler_params=_params(1),
        name="dft_pos_prompt",
    )(ct, st, ab)


def _dft_pos_sample(ab, ct, st):
    tr = 1024
    row_blk0 = NP_TOK // DEC_SEQ
    return pl.pallas_call(
        _dft_pos_kernel,
        grid=(DEC_BATCH, N_FGROUPS, DEC_SEQ // tr),
        in_specs=[pl.BlockSpec((tr, DEC_SEQ), lambda b, g, t: (t, 0)),
                  pl.BlockSpec((tr, DEC_SEQ), lambda b, g, t: (t, 0)),
                  pl.BlockSpec((DEC_SEQ, 2 * FGROUP_DIM), lambda b, g, t: (row_blk0 + b, g))],
        out_specs=pl.BlockSpec((tr, FGROUP_DIM), lambda b, g, t: (b * (DEC_SEQ // tr) + t, g)),
        out_shape=jax.ShapeDtypeStruct((NS_TOK, D_FOURIER), BF16),
        compiler_params=_params(3),
        name="dft_pos_sample",
    )(ct, st, ab)


def _wout_kernel(a_ref, f_ref, w_ref, x_ref, g_ref, tiled_ref, o_ref, heads_ref, wb_ref):
    @pl.when(pl.program_id(1) == 0)
    def _():
        _cast_weight(w_ref, wb_ref)

    acc = (jnp.dot(a_ref[...], wb_ref[0:D_ATT, :], preferred_element_type=F32)
           + jnp.dot(f_ref[...], wb_ref[D_ATT:D_ATT + D_FOURIER, :], preferred_element_type=F32))
    o_ref[...] = x_ref[...] + g_ref[0] * acc

    for tt in range(tiled_ref.shape[0] // (N_HEADS * SUBLANES)):
        for s in range(SUBLANES):
            for hb in range(N_HEADS // SUBLANES):
                src = pl.ds(tt * (N_HEADS * SUBLANES) + hb * SUBLANES * SUBLANES + s, SUBLANES,
                            stride=SUBLANES)
                dst = pl.ds((tt * SUBLANES + s) * N_HEADS + hb * SUBLANES, SUBLANES)
                heads_ref[dst, :] = tiled_ref[src, :]


def _wout(att, fou, w_out, x, mod3, tiled, *, row0, name):
    assert HEAD_DIM == LANES and N_HEADS % SUBLANES == 0
    tm, tn = 1024, 512
    n_rows = x.shape[0]
    blk0 = row0 // tm
    nj, ni = D_MODEL // tn, n_rows // tm
    flat = tiled.reshape(-1, LANES)
    slab = flat.shape[0] // (nj * ni)
    assert slab * nj * ni == flat.shape[0] and slab % (N_HEADS * SUBLANES) == 0
    return pl.pallas_call(
        _wout_kernel,
        grid=(nj, ni),
        in_specs=[pl.BlockSpec((tm, D_ATT), lambda j, i: (i, 0)),
                  pl.BlockSpec((tm, D_FOURIER), lambda j, i: (i, 0)),
                  pl.BlockSpec((D_MODEL, tn), lambda j, i: (0, j)),
                  pl.BlockSpec((tm, tn), lambda j, i: (i, j)),
                  pl.BlockSpec((1, 1, tn), lambda j, i: (_mod_row(blk0 + i, tm) * 6 + 2, 0, j)),
                  pl.BlockSpec((slab, LANES), lambda j, i: (j * ni + i, 0))],
        out_specs=[pl.BlockSpec((tm, tn), lambda j, i: (i, j)),
                   pl.BlockSpec((slab, LANES), lambda j, i: (j * ni + i, 0))],
        out_shape=[jax.ShapeDtypeStruct((n_rows, D_MODEL), F32),
                   jax.ShapeDtypeStruct(flat.shape, F32)],
        scratch_shapes=[pltpu.VMEM((D_MODEL, tn), BF16)],
        compiler_params=_params(2),
        name=name,
    )(att, fou, w_out, x, mod3, flat)


FF_TN = 256


def _gate_up_kernel(h_ref, wg_ref, wu_ref, wd_ref, o_ref, wdb_ref, wgb_ref, wub_ref, *, n_real):
    j = pl.program_id(0)

    @pl.when((pl.program_id(1) == 0) & (j < n_real))
    def _():
        _cast_weight(wg_ref, wgb_ref)
        _cast_weight(wu_ref, wub_ref)

    @pl.when(j < n_real)
    def _():
        for c in range(h_ref.shape[0] // FF_ROWS):
            rs = slice(c * FF_ROWS, (c + 1) * FF_ROWS)
            h = h_ref[rs, :]
            g = jnp.dot(h, wgb_ref[...], preferred_element_type=F32)
            u = jnp.dot(h, wub_ref[...], preferred_element_type=F32)
            o_ref[rs, :] = (g * jax.nn.sigmoid(g) * u).astype(o_ref.dtype)
        wdb_ref[...] = wd_ref[...].astype(wdb_ref.dtype)

    @pl.when(j >= n_real)
    def _():
        o_ref[...] = jnp.zeros_like(o_ref)
        wdb_ref[...] = jnp.zeros_like(wdb_ref)


FF_ROWS = 512


def _gate_up(h2, w_gate, w_up, w_down):
    tm, tn = 2048, FF_TN
    n_real = D_FF // tn
    ni = N_TOK // tm
    wd_rows = D_FF // (n_real * ni)
    assert wd_rows * n_real * ni == D_FF and wd_rows % 16 == 0
    assert (D_FF_PAD - D_FF) == (D_FF_PAD // tn - n_real) * ni * wd_rows
    wmap = lambda j, i: (0, jnp.minimum(j, n_real - 1))
    return pl.pallas_call(
        functools.partial(_gate_up_kernel, n_real=n_real),
        grid=(D_FF_PAD // tn, ni),
        in_specs=[pl.BlockSpec((tm, D_MODEL), lambda j, i: (i, 0)),
                  pl.BlockSpec((D_MODEL, tn), wmap),
                  pl.BlockSpec((D_MODEL, tn), wmap),
                  pl.BlockSpec((wd_rows, D_MODEL),
                               lambda j, i: (jnp.minimum(j * ni + i, n_real * ni - 1), 0))],
        out_specs=[pl.BlockSpec((tm, tn), lambda j, i: (i, j)),
                   pl.BlockSpec((wd_rows, D_MODEL), lambda j, i: (j * ni + i, 0))],
        out_shape=[jax.ShapeDtypeStruct((N_TOK, D_FF_PAD), BF16),
                   jax.ShapeDtypeStruct((D_FF_PAD, D_MODEL), BF16)],
        scratch_shapes=[pltpu.VMEM((D_MODEL, tn), BF16), pltpu.VMEM((D_MODEL, tn), BF16)],
        compiler_params=_params(2),
        name="ffn_gate_up",
    )(h2, w_gate, w_up, w_down)


def _down_kernel(a_ref, w_ref, x_ref, g_ref, fg_ref, o_ref, r_ref, *, nk):
    k = pl.program_id(1)

    def accumulate(first):
        for n in range(D_MODEL // DOWN_TN):
            ns = slice(n * DOWN_TN, (n + 1) * DOWN_TN)
            part = jnp.dot(a_ref[...], w_ref[:, ns], preferred_element_type=F32)
            if first:
                o_ref[:, ns] = part
            else:
                o_ref[:, ns] += part

    @pl.when(k == 0)
    def _():
        accumulate(True)

    @pl.when(k > 0)
    def _():
        accumulate(False)

    @pl.when(k == nk - 1)
    def _():
        lanes = 128

        def residual_stats(c, carry):
            rows = pl.ds(pl.multiple_of(c * APPLY_ROWS, APPLY_ROWS), APPLY_ROWS)
            ss = jnp.zeros((APPLY_ROWS, lanes), F32)
            for j in range(D_MODEL // APPLY_COLS):
                cols = slice(j * APPLY_COLS, (j + 1) * APPLY_COLS)
                x2 = x_ref[rows, cols] + g_ref[0, :, cols] * o_ref[rows, cols]
                o_ref[rows, cols] = x2
                sq = x2 * x2
                for q in range(APPLY_COLS // lanes):
                    ss = ss + sq[:, q * lanes:(q + 1) * lanes]
            ms = jnp.sum(ss, axis=-1, keepdims=True) * (1.0 / D_MODEL)
            r_ref[rows, :] = lax.rsqrt(ms + EPS)
            return carry
        lax.fori_loop(0, o_ref.shape[0] // APPLY_ROWS, residual_stats, 0)

        def normalise(rows, cols):
            o_ref[rows, cols] = o_ref[rows, cols] * r_ref[rows, :] * fg_ref[:, cols]
        _row_col_blocks(o_ref.shape[0], o_ref.shape[1], normalise)


DOWN_TN = 512


def _down(a, wd, x1, mod3, final_g, *, row0, name):
    tm, tk = 512, 1024
    nk = D_FF_PAD // tk
    n_rows = x1.shape[0]
    blk0 = row0 // tm
    return pl.pallas_call(
        functools.partial(_down_kernel, nk=nk),
        grid=(n_rows // tm, nk),
        in_specs=[pl.BlockSpec((tm, tk), lambda i, k: (blk0 + i, k)),
                  pl.BlockSpec((tk, D_MODEL), lambda i, k: (k, 0)),
                  pl.BlockSpec((tm, D_MODEL), lambda i, k: (i, 0)),
                  pl.BlockSpec((1, 1, D_MODEL), lambda i, k: (_mod_row(blk0 + i, tm) * 6 + 5, 0, 0)),
                  pl.BlockSpec((1, D_MODEL), lambda i, k: (0, 0))],
        out_specs=pl.BlockSpec((tm, D_MODEL), lambda i, k: (i, 0)),
        out_shape=jax.ShapeDtypeStruct((n_rows, D_MODEL), F32),
        scratch_shapes=[pltpu.VMEM((tm, 1), F32)],
        compiler_params=_params(2),
        name=name,
    )(a, wd, x1, mod3, final_g)


def kernel(x_prompt, x_sample, cache_k, cache_v, c, c_ctx, w_ada, b_ada, norm1_g, w_in, rpb,
           w_out, norm2_g, w_gate, w_up, w_down, final_g):
    xp = x_prompt.reshape(NP_TOK, D_MODEL)
    xs = x_sample.reshape(NS_TOK, D_MODEL)

    cvec = jnp.concatenate([c_ctx[None, :], c, jnp.zeros((N_MOD - 1 - DEC_BATCH, D_MODEL), F32)], axis=0)
    mod = _ada(cvec, w_ada[0], b_ada[0][None, :])
    mod3 = mod.reshape(N_MOD * 6, 1, D_MODEL)

    h = _norm_modulate(xp, xs, norm1_g[0][None, :], mod3, shift_part=0, scale_part=1,
                       name="norm1_mod")

    w_in0 = w_in[0]
    tm, tn = 1024, 512
    n_att_blks = D_ATT // tn
    tm2 = 2 * tm
    qu = _ws_matmul(h, w_in0, row_blk0=0, n_row_blks=N_TOK // tm2,
                    col_map=lambda j: jnp.where(j < n_att_blks, j, j + 2 * n_att_blks),
                    n_col_blks=2 * n_att_blks, tm=tm2, tn=tn, out_dtype=BF16, name="w_in_q_u",
                    vmem=VMEM_LIMIT_BIG)
    kv_s = _ws_matmul(h, w_in0, row_blk0=NP_TOK // tm2, n_row_blks=NS_TOK // tm2,
                      col_map=lambda j: j + n_att_blks, n_col_blks=2 * n_att_blks,
                      tm=tm2, tn=tn, out_dtype=BF16, name="w_in_kv_sample", vmem=VMEM_LIMIT_BIG)
    newk = _ws_matmul(h, w_in0, row_blk0=0, n_row_blks=NP_TOK // tm,
                      col_map=lambda j: j + n_att_blks, n_col_blks=n_att_blks,
                      tm=tm, tn=tn, out_dtype=F32, name="w_in_k_prompt", tiled_out=True)
    newv = _ws_matmul(h, w_in0, row_blk0=0, n_row_blks=NP_TOK // tm,
                      col_map=lambda j: j + 2 * n_att_blks, n_col_blks=n_att_blks,
                      tm=tm, tn=tn, out_dtype=F32, name="w_in_v_prompt", tiled_out=True)

    cls_of_g, patterns = _nbr_classes()
    bias = _rpb_bias(rpb[0].reshape(-1), patterns)
    ck = cache_k[:, 0].reshape(DEC_BATCH * PAST_LEN, D_ATT)
    cv = cache_v[:, 0].reshape(DEC_BATCH * PAST_LEN, D_ATT)
    att_s = _nbr_attention(qu, kv_s, ck, cv, bias, cls_of_g)
    att_p = _ctx_attention(qu, newk, newv)

    cc, sc = _dft_tables(FGROUP_DIM)
    w1 = jnp.asarray(np.concatenate([cc, -sc], axis=1)).astype(BF16)
    ab = _dft_chan(qu, w1)
    ctp, stp = _dft_tables(SEQ)
    fou_p = _dft_pos_prompt(ab, jnp.asarray(ctp).astype(BF16), jnp.asarray(stp).astype(BF16))
    cts, sts = _dft_tables(DEC_SEQ)
    fou_s = _dft_pos_sample(ab, jnp.asarray(cts).astype(BF16), jnp.asarray(sts).astype(BF16))

    x1p, k_heads = _wout(att_p, fou_p, w_out[0], xp, mod3, newk, row0=0, name="w_out_prompt")
    x1s, v_heads = _wout(att_s, fou_s, w_out[0], xs, mod3, newv, row0=NP_TOK, name="w_out_sample")

    h2 = _norm_modulate(x1p, x1s, norm2_g[0][None, :], mod3, shift_part=3, scale_part=4,
                        name="norm2_mod")
    a, wd = _gate_up(h2, w_gate[0], w_up[0], w_down[0])
    fg = final_g[None, :]
    y_prompt = _down(a, wd, x1p, mod3, fg, row0=0,
                     name="ffn_down_prompt").reshape(BATCH, SEQ, D_MODEL)
    y_sample = _down(a, wd, x1s, mod3, fg, row0=NP_TOK,
                     name="ffn_down_sample").reshape(DEC_BATCH, DEC_SEQ, D_MODEL)
    new_cache_k = k_heads.reshape(BATCH, 1, SEQ, N_HEADS, HEAD_DIM)
    new_cache_v = v_heads.reshape(BATCH, 1, SEQ, N_HEADS, HEAD_DIM)
    return (y_prompt, y_sample, new_cache_k, new_cache_v)
```

```python
import functools

import numpy as np
import jax
import jax.numpy as jnp
from jax import lax
from jax.experimental import pallas as pl
from jax.experimental.pallas import tpu as pltpu

F32 = jnp.float32
BF16 = jnp.bfloat16

D_MODEL = 4096
BATCH = 32
SEQ = 256
DEC_BATCH = 4
DEC_SEQ = 2048
PAST_LEN = 256
GRID_W = 64
GRID_ROWS = DEC_SEQ // GRID_W
D_ATT = 2048
D_FOURIER = 2048
HEAD_DIM = 128
N_HEADS = 16
N_FGROUPS = 4
FGROUP_DIM = 512
WIN_H = 8
WIN_W = 16
D_FF = 11008
D_FF_PAD = 11264
EPS = 1e-6
NEG_INF = -1e30
SCALE = HEAD_DIM ** -0.5
LOG2E = 1.4426950408889634
SCALE2 = SCALE * LOG2E

NP_TOK = BATCH * SEQ
NS_TOK = DEC_BATCH * DEC_SEQ
N_TOK = NP_TOK + NS_TOK
N_MOD = 8

VMEM_LIMIT = 56 * 1024 * 1024
VMEM_LIMIT_BIG = 60 * 1024 * 1024


def _params(n_axes, vmem=VMEM_LIMIT):
    return pltpu.CompilerParams(dimension_semantics=("arbitrary",) * n_axes,
                                vmem_limit_bytes=vmem)


def _mod_row(i, tm):
    nh = NP_TOK // tm
    return jnp.where(i < nh, 0, 1 + (i - nh) // (DEC_SEQ // tm))


def _ada_kernel(c_ref, w_ref, b_ref, o_ref):
    c = c_ref[...]
    s = (c * jax.nn.sigmoid(c)).astype(BF16)
    o_ref[...] = jnp.dot(s, w_ref[...].astype(BF16), preferred_element_type=F32) + b_ref[...]


def _ada(cvec, w_ada, b_ada):
    tn = 512
    n = w_ada.shape[1]
    return pl.pallas_call(
        _ada_kernel,
        grid=(n // tn,),
        in_specs=[pl.BlockSpec((N_MOD, D_MODEL), lambda j: (0, 0)),
                  pl.BlockSpec((D_MODEL, tn), lambda j: (0, j)),
                  pl.BlockSpec((1, tn), lambda j: (0, j))],
        out_specs=pl.BlockSpec((N_MOD, tn), lambda j: (0, j)),
        out_shape=jax.ShapeDtypeStruct((N_MOD, n), F32),
        compiler_params=_params(1),
        name="ada_mod",
    )(cvec, w_ada, b_ada)


ROW_CHUNK = 8


STATS_UNROLL = 8
APPLY_ROWS = 64
APPLY_COLS = 512


def _row_chunks(n_rows, fn):
    def body(c, carry):
        fn(pl.ds(pl.multiple_of(c * ROW_CHUNK, ROW_CHUNK), ROW_CHUNK))
        return carry
    lax.fori_loop(0, n_rows // ROW_CHUNK, body, 0, unroll=STATS_UNROLL)


def _row_col_blocks(n_rows, n_cols, fn):
    def body(c, carry):
        rows = pl.ds(pl.multiple_of(c * APPLY_ROWS, APPLY_ROWS), APPLY_ROWS)
        for j in range(n_cols // APPLY_COLS):
            fn(rows, slice(j * APPLY_COLS, (j + 1) * APPLY_COLS))
        return carry
    lax.fori_loop(0, n_rows // APPLY_ROWS, body, 0)


def _norm_mod(x_ref, g_ref, sh_ref, sc_ref, o_ref, r_ref, gs_ref):
    gs_ref[...] = g_ref[...] * (1.0 + sc_ref[0])

    def stats(rows):
        x = x_ref[rows, :]
        r_ref[rows, :] = lax.rsqrt(jnp.mean(x * x, axis=-1, keepdims=True) + EPS)
    _row_chunks(o_ref.shape[0], stats)

    def apply(rows, cols):
        o_ref[rows, cols] = (x_ref[rows, cols] * r_ref[rows, :] * gs_ref[:, cols]
                             + sh_ref[0, :, cols]).astype(o_ref.dtype)
    _row_col_blocks(o_ref.shape[0], o_ref.shape[1], apply)


def _norm_mod2_kernel(xp_ref, xs_ref, g_ref, sh_ref, sc_ref, o_ref, r_ref, gs_ref, *, nh):
    i = pl.program_id(0)

    @pl.when(i < nh)
    def _():
        _norm_mod(xp_ref, g_ref, sh_ref, sc_ref, o_ref, r_ref, gs_ref)

    @pl.when(i >= nh)
    def _():
        _norm_mod(xs_ref, g_ref, sh_ref, sc_ref, o_ref, r_ref, gs_ref)


def _mod_spec(part, tm):
    return pl.BlockSpec((1, 1, D_MODEL), lambda i: (_mod_row(i, tm) * 6 + part, 0, 0))


def _norm_modulate(xp, xs, g, mod3, *, shift_part, scale_part, name):
    tm = 512
    nh = NP_TOK // tm
    return pl.pallas_call(
        functools.partial(_norm_mod2_kernel, nh=nh),
        grid=(N_TOK // tm,),
        in_specs=[pl.BlockSpec((tm, D_MODEL), lambda i: (jnp.minimum(i, nh - 1), 0)),
                  pl.BlockSpec((tm, D_MODEL), lambda i: (jnp.maximum(i - nh, 0), 0)),
                  pl.BlockSpec((1, D_MODEL), lambda i: (0, 0)),
                  _mod_spec(shift_part, tm), _mod_spec(scale_part, tm)],
        out_specs=pl.BlockSpec((tm, D_MODEL), lambda i: (i, 0)),
        out_shape=jax.ShapeDtypeStruct((N_TOK, D_MODEL), BF16),
        scratch_shapes=[pltpu.VMEM((tm, 1), F32), pltpu.VMEM((1, D_MODEL), F32)],
        compiler_params=_params(1),
        name=name,
    )(xp, xs, g, mod3, mod3)


CAST_ROWS = 512


def _cast_weight(w_ref, wb_ref):
    rows = min(CAST_ROWS, w_ref.shape[0])

    def body(c, carry):
        r = pl.multiple_of(c * rows, rows)
        wb_ref[pl.ds(r, rows), :] = w_ref[pl.ds(r, rows), :].astype(BF16)
        return carry
    lax.fori_loop(0, w_ref.shape[0] // rows, body, 0)


WS_ROWS = 512


def _ws_kernel(x_ref, w_ref, o_ref, wb_ref):
    @pl.when(pl.program_id(1) == 0)
    def _():
        _cast_weight(w_ref, wb_ref)

    for c in range(x_ref.shape[0] // WS_ROWS):
        rs = slice(c * WS_ROWS, (c + 1) * WS_ROWS)
        res = jnp.dot(x_ref[rs, :], wb_ref[...], preferred_element_type=F32).astype(o_ref.dtype)
        if len(o_ref.shape) == 2:
            o_ref[rs, :] = res
        else:
            ts = slice(c * WS_ROWS // SUBLANES, (c + 1) * WS_ROWS // SUBLANES)
            for t in range(o_ref.shape[1]):
                o_ref[ts, t] = res[:, t * LANES:(t + 1) * LANES].reshape(WS_ROWS // SUBLANES, SUBLANES, LANES)


SUBLANES, LANES = 8, 128


def _ws_matmul(x, w, *, row_blk0, n_row_blks, col_map, n_col_blks, tm, tn, out_dtype, name,
               vmem=VMEM_LIMIT, tiled_out=False):
    k = x.shape[1]
    if tiled_out:
        out_spec = pl.BlockSpec((tm // SUBLANES, tn // LANES, SUBLANES, LANES), lambda j, i: (i, j, 0, 0))
        out_shape = (n_row_blks * tm // SUBLANES, n_col_blks * tn // LANES, SUBLANES, LANES)
    else:
        out_spec = pl.BlockSpec((tm, tn), lambda j, i: (i, j))
        out_shape = (n_row_blks * tm, n_col_blks * tn)
    return pl.pallas_call(
        _ws_kernel,
        grid=(n_col_blks, n_row_blks),
        in_specs=[pl.BlockSpec((tm, k), lambda j, i: (row_blk0 + i, 0)),
                  pl.BlockSpec((k, tn), lambda j, i: (0, col_map(j)))],
        out_specs=out_spec,
        out_shape=jax.ShapeDtypeStruct(out_shape, out_dtype),
        scratch_shapes=[pltpu.VMEM((k, tn), BF16)],
        compiler_params=_params(2, vmem),
        name=name,
    )(x, w)


def _ctx_attn_kernel(q_ref, k_ref, v_ref, o_ref):
    for h in range(N_HEADS):
        hs = slice(h * HEAD_DIM, (h + 1) * HEAD_DIM)
        q = q_ref[:, hs]
        k = k_ref[:, h].reshape(SEQ, HEAD_DIM).astype(BF16)
        v = v_ref[:, h].reshape(SEQ, HEAD_DIM).astype(BF16)
        s = lax.dot_general(q, k, (((1,), (1,)), ((), ())), preferred_element_type=F32) * SCALE2
        m = jnp.max(s, axis=-1, keepdims=True)
        p = jnp.exp2(s - m)
        l = jnp.sum(p, axis=-1, keepdims=True)
        o = jnp.dot(p.astype(BF16), v, preferred_element_type=F32) / l
        o_ref[:, hs] = o.astype(o_ref.dtype)


def _ctx_attention(qu, newk, newv):
    return pl.pallas_call(
        _ctx_attn_kernel,
        grid=(BATCH,),
        in_specs=[pl.BlockSpec((SEQ, D_ATT), lambda b: (b, 0)),
                  pl.BlockSpec((SEQ // SUBLANES, N_HEADS, SUBLANES, LANES), lambda b: (b, 0, 0, 0)),
                  pl.BlockSpec((SEQ // SUBLANES, N_HEADS, SUBLANES, LANES), lambda b: (b, 0, 0, 0))],
        out_specs=pl.BlockSpec((SEQ, D_ATT), lambda b: (b, 0)),
        out_shape=jax.ShapeDtypeStruct((NP_TOK, D_ATT), BF16),
        compiler_params=_params(1),
        name="ctx_attention",
    )(qu, newk, newv)


NA_QROWS = 4
NA_KROWS = NA_QROWS + WIN_H
NA_Q = NA_QROWS * GRID_W
NA_K = NA_KROWS * GRID_W
NA_GROUPS = GRID_ROWS // NA_QROWS


def _nbr_window_start(g):
    lo, hi = 0, GRID_ROWS - NA_KROWS
    if isinstance(g, int):
        return min(max(NA_QROWS * g - WIN_H // 2, lo), hi)
    return jnp.clip(NA_QROWS * g - WIN_H // 2, lo, hi)


def _nbr_classes():
    patterns, cls_of_g = [], []
    for g in range(NA_GROUPS):
        start = _nbr_window_start(g)
        pat = []
        for i in range(NA_QROWS):
            r = NA_QROWS * g + i
            rstart = min(max(r - WIN_H // 2, 0), GRID_ROWS - WIN_H)
            pat.append(tuple((start + j - r + WIN_H - 1) if rstart <= start + j < rstart + WIN_H else None
                             for j in range(NA_KROWS)))
        pat = tuple(pat)
        if pat not in patterns:
            patterns.append(pat)
        cls_of_g.append(patterns.index(pat))
    return tuple(cls_of_g), tuple(patterns)


def _rpb_bias_kernel(rpb_ref, o_ref, t_ref, *, patterns):
    h = pl.program_id(0)
    qc = lax.broadcasted_iota(jnp.int32, (GRID_W, GRID_W), 0)
    kc = lax.broadcasted_iota(jnp.int32, (GRID_W, GRID_W), 1)
    dc = jnp.clip(kc - qc + (WIN_W - 1), 0, 2 * WIN_W - 2)
    cstart = jnp.clip(qc - WIN_W // 2, 0, GRID_W - WIN_W)
    mask = (kc >= cstart) & (kc < cstart + WIN_W)
    n_dc = 2 * WIN_W - 1
    n_dr = 2 * WIN_H - 1
    for dr in range(n_dr):
        t = jnp.zeros((GRID_W, GRID_W), F32)
        for d in range(n_dc):
            t = jnp.where(dc == d, rpb_ref[h * (n_dr * n_dc) + dr * n_dc + d], t)
        t_ref[dr] = jnp.where(mask, t * LOG2E, NEG_INF)
    outside = jnp.full((GRID_W, GRID_W), NEG_INF, F32)
    for c, pat in enumerate(patterns):
        for i in range(NA_QROWS):
            for j in range(NA_KROWS):
                dr = pat[i][j]
                o_ref[c, 0, i * GRID_W:(i + 1) * GRID_W, j * GRID_W:(j + 1) * GRID_W] = (
                    outside if dr is None else t_ref[dr])


def _rpb_bias(rpb_flat, patterns):
    n_cls = len(patterns)
    return pl.pallas_call(
        functools.partial(_rpb_bias_kernel, patterns=patterns),
        grid=(N_HEADS,),
        in_specs=[pl.BlockSpec(memory_space=pltpu.SMEM)],
        out_specs=pl.BlockSpec((n_cls, 1, NA_Q, NA_K), lambda h: (0, h, 0, 0)),
        out_shape=jax.ShapeDtypeStruct((n_cls, N_HEADS, NA_Q, NA_K), F32),
        scratch_shapes=[pltpu.VMEM((2 * WIN_H - 1, GRID_W, GRID_W), F32)],
        compiler_params=_params(1),
        name="rpb_bias",
    )(rpb_flat)


NA_HEADS = 4
NA_COLS = NA_HEADS * HEAD_DIM


def _nbr_attn_kernel(q_ref, k_ref, v_ref, kc_ref, vc_ref, bias_ref, w_ref, o_ref, wb_ref,
                     kcb_ref, vcb_ref, *, cls_of_g):
    kcb_ref[...] = kc_ref[...].astype(BF16)
    vcb_ref[...] = vc_ref[...].astype(BF16)
    _cast_weight(w_ref, wb_ref)
    dn = (((1,), (1,)), ((), ()))

    def group_body(g, carry):
        cls = jnp.int32(cls_of_g[0])
        for gg in range(1, NA_GROUPS):
            if cls_of_g[gg] != cls_of_g[gg - 1]:
                cls = jnp.where(g >= gg, cls_of_g[gg], cls)
        q0 = pl.multiple_of(g * NA_Q, NA_Q)
        k0 = pl.multiple_of(_nbr_window_start(g) * GRID_W, GRID_W)
        for h in range(NA_HEADS):
            hs = slice(h * HEAD_DIM, (h + 1) * HEAD_DIM)
            q = q_ref[pl.ds(q0, NA_Q), hs]
            kw = k_ref[pl.ds(k0, NA_K), hs]
            vw = v_ref[pl.ds(k0, NA_K), hs]
            s_loc = lax.dot_general(q, kw, dn, preferred_element_type=F32) * SCALE2 + bias_ref[cls, h]
            s_ctx = lax.dot_general(q, kcb_ref[:, hs], dn, preferred_element_type=F32) * SCALE2
            m = jnp.maximum(jnp.max(s_loc, axis=-1, keepdims=True),
                            jnp.max(s_ctx, axis=-1, keepdims=True))
            p_loc = jnp.exp2(s_loc - m)
            p_ctx = jnp.exp2(s_ctx - m)
            l = jnp.sum(p_loc, axis=-1, keepdims=True) + jnp.sum(p_ctx, axis=-1, keepdims=True)
            o = (jnp.dot(p_loc.astype(BF16), vw, preferred_element_type=F32)
                 + jnp.dot(p_ctx.astype(BF16), vcb_ref[:, hs], preferred_element_type=F32)) / l
            o_ref[pl.ds(q0, NA_Q), hs] = o.astype(o_ref.dtype)
        return carry

    lax.fori_loop(0, NA_GROUPS, group_body, 0)


def _nbr_attention(qu, kv_s, ck, cv, bias, cls_of_g, w_out):
    n_hg = N_HEADS // NA_HEADS
    n_cls = bias.shape[0]
    row_blk0 = NP_TOK // DEC_SEQ
    slab = w_out.shape[0] // (n_hg * DEC_BATCH)
    assert slab * n_hg * DEC_BATCH == w_out.shape[0] and slab % 16 == 0
    return pl.pallas_call(
        functools.partial(_nbr_attn_kernel, cls_of_g=cls_of_g),
        grid=(n_hg, DEC_BATCH),
        in_specs=[pl.BlockSpec((DEC_SEQ, NA_COLS), lambda g, b: (row_blk0 + b, g)),
                  pl.BlockSpec((DEC_SEQ, NA_COLS), lambda g, b: (b, g)),
                  pl.BlockSpec((DEC_SEQ, NA_COLS), lambda g, b: (b, n_hg + g)),
                  pl.BlockSpec((PAST_LEN, NA_COLS), lambda g, b: (b, g)),
                  pl.BlockSpec((PAST_LEN, NA_COLS), lambda g, b: (b, g)),
                  pl.BlockSpec((n_cls, NA_HEADS, NA_Q, NA_K), lambda g, b: (0, g, 0, 0)),
                  pl.BlockSpec((slab, w_out.shape[1]), lambda g, b: (g * DEC_BATCH + b, 0))],
        out_specs=[pl.BlockSpec((DEC_SEQ, NA_COLS), lambda g, b: (b, g)),
                   pl.BlockSpec((slab, w_out.shape[1]), lambda g, b: (g * DEC_BATCH + b, 0))],
        out_shape=[jax.ShapeDtypeStruct((NS_TOK, D_ATT), BF16),
                   jax.ShapeDtypeStruct(w_out.shape, BF16)],
        scratch_shapes=[pltpu.VMEM((PAST_LEN, NA_COLS), BF16),
                        pltpu.VMEM((PAST_LEN, NA_COLS), BF16)],
        compiler_params=_params(2),
        name="nbr_attention",
    )(qu, kv_s, kv_s, ck, cv, bias, w_out)


def _dft_tables(n):
    idx = np.arange(n, dtype=np.int64)
    ang = (2.0 * np.pi / n) * ((idx[:, None] * idx[None, :]) % n).astype(np.float64)
    return ((np.cos(ang) / np.sqrt(n)).astype(np.float32),
            (-np.sin(ang) / np.sqrt(n)).astype(np.float32))


def _dft_chan_kernel(u_ref, w_ref, o_ref):
    for g in range(N_FGROUPS):
        u = u_ref[:, g * FGROUP_DIM:(g + 1) * FGROUP_DIM]
        o_ref[:, g * 2 * FGROUP_DIM:(g + 1) * 2 * FGROUP_DIM] = jnp.dot(
            u, w_ref[...], preferred_element_type=F32).astype(o_ref.dtype)


def _dft_chan(qu, w1):
    tm = 1024
    return pl.pallas_call(
        _dft_chan_kernel,
        grid=(N_TOK // tm,),
        in_specs=[pl.BlockSpec((tm, D_FOURIER), lambda i: (i, 1)),
                  pl.BlockSpec((FGROUP_DIM, 2 * FGROUP_DIM), lambda i: (0, 0))],
        out_specs=pl.BlockSpec((tm, 2 * D_FOURIER), lambda i: (i, 0)),
        out_shape=jax.ShapeDtypeStruct((N_TOK, 2 * D_FOURIER), BF16),
        compiler_params=_params(1),
        name="dft_channels",
    )(qu, w1)


def _dft_pos_kernel(ct_ref, st_ref, ab_ref, o_ref):
    n_in = ct_ref.shape[1]
    n_out = ct_ref.shape[0]
    for s in range(ab_ref.shape[0] // n_in):
        for g in range(ab_ref.shape[1] // (2 * FGROUP_DIM)):
            rows = slice(s * n_in, (s + 1) * n_in)
            a = ab_ref[rows, g * 2 * FGROUP_DIM:g * 2 * FGROUP_DIM + FGROUP_DIM]
            b = ab_ref[rows, g * 2 * FGROUP_DIM + FGROUP_DIM:(g + 1) * 2 * FGROUP_DIM]
            o = (jnp.dot(ct_ref[...], a, preferred_element_type=F32)
                 + jnp.dot(st_ref[...], b, preferred_element_type=F32))
            o_ref[s * n_out:(s + 1) * n_out, g * FGROUP_DIM:(g + 1) * FGROUP_DIM] = o.astype(o_ref.dtype)


def _dft_pos_prompt(ab, ct, st):
    n_seq = 4
    rows = n_seq * SEQ
    return pl.pallas_call(
        _dft_pos_kernel,
        grid=(BATCH // n_seq,),
        in_specs=[pl.BlockSpec((SEQ, SEQ), lambda b: (0, 0)),
                  pl.BlockSpec((SEQ, SEQ), lambda b: (0, 0)),
                  pl.BlockSpec((rows, 2 * D_FOURIER), lambda b: (b, 0))],
        out_specs=pl.BlockSpec((rows, D_FOURIER), lambda b: (b, 0)),
        out_shape=jax.ShapeDtypeStruct((NP_TOK, D_FOURIER), BF16),
        compiler_params=_params(1),
        name="dft_pos_prompt",
    )(ct, st, ab)


def _dft_pos_sample(ab, ct, st):
    tr = 1024
    row_blk0 = NP_TOK // DEC_SEQ
    return pl.pallas_call(
        _dft_pos_kernel,
        grid=(DEC_BATCH, N_FGROUPS, DEC_SEQ // tr),
        in_specs=[pl.BlockSpec((tr, DEC_SEQ), lambda b, g, t: (t, 0)),
                  pl.BlockSpec((tr, DEC_SEQ), lambda b, g, t: (t, 0)),
                  pl.BlockSpec((DEC_SEQ, 2 * FGROUP_DIM), lambda b, g, t: (row_blk0 + b, g))],
        out_specs=pl.BlockSpec((tr, FGROUP_DIM), lambda b, g, t: (b * (DEC_SEQ // tr) + t, g)),
        out_shape=jax.ShapeDtypeStruct((NS_TOK, D_FOURIER), BF16),
        compiler_params=_params(3),
        name="dft_pos_sample",
    )(ct, st, ab)


def _wout_kernel(a_ref, f_ref, w_ref, x_ref, g_ref, tiled_ref, o_ref, heads_ref):
    for c in range(a_ref.shape[0] // WS_ROWS):
        rs = slice(c * WS_ROWS, (c + 1) * WS_ROWS)
        acc = (jnp.dot(a_ref[rs, :], w_ref[0:D_ATT, :], preferred_element_type=F32)
               + jnp.dot(f_ref[rs, :], w_ref[D_ATT:D_ATT + D_FOURIER, :], preferred_element_type=F32))
        o_ref[rs, :] = x_ref[rs, :] + g_ref[0] * acc

    for tt in range(tiled_ref.shape[0] // (N_HEADS * SUBLANES)):
        for s in range(SUBLANES):
            for hb in range(N_HEADS // SUBLANES):
                src = pl.ds(tt * (N_HEADS * SUBLANES) + hb * SUBLANES * SUBLANES + s, SUBLANES,
                            stride=SUBLANES)
                dst = pl.ds((tt * SUBLANES + s) * N_HEADS + hb * SUBLANES, SUBLANES)
                heads_ref[dst, :] = tiled_ref[src, :]


def _wout(att, fou, w_out, x, mod3, tiled, *, row0, name):
    assert HEAD_DIM == LANES and N_HEADS % SUBLANES == 0
    tm, tn = 1024, 1024
    n_rows = x.shape[0]
    blk0 = row0 // tm
    nj, ni = D_MODEL // tn, n_rows // tm
    flat = tiled.reshape(-1, LANES)
    slab = flat.shape[0] // (nj * ni)
    assert slab * nj * ni == flat.shape[0] and slab % (N_HEADS * SUBLANES) == 0
    return pl.pallas_call(
        _wout_kernel,
        grid=(nj, ni),
        in_specs=[pl.BlockSpec((tm, D_ATT), lambda j, i: (i, 0)),
                  pl.BlockSpec((tm, D_FOURIER), lambda j, i: (i, 0)),
                  pl.BlockSpec((D_MODEL, tn), lambda j, i: (0, j)),
                  pl.BlockSpec((tm, tn), lambda j, i: (i, j)),
                  pl.BlockSpec((1, 1, tn), lambda j, i: (_mod_row(blk0 + i, tm) * 6 + 2, 0, j)),
                  pl.BlockSpec((slab, LANES), lambda j, i: (j * ni + i, 0))],
        out_specs=[pl.BlockSpec((tm, tn), lambda j, i: (i, j)),
                   pl.BlockSpec((slab, LANES), lambda j, i: (j * ni + i, 0))],
        out_shape=[jax.ShapeDtypeStruct((n_rows, D_MODEL), F32),
                   jax.ShapeDtypeStruct(flat.shape, F32)],
        compiler_params=_params(2, VMEM_LIMIT_BIG),
        name=name,
    )(att, fou, w_out, x, mod3, flat)


FF_TN = 256


def _gate_up_kernel(h_ref, wg_ref, wu_ref, wd_ref, o_ref, wdb_ref, wgb_ref, wub_ref, *, n_real):
    j = pl.program_id(0)

    @pl.when((pl.program_id(1) == 0) & (j < n_real))
    def _():
        _cast_weight(wg_ref, wgb_ref)
        _cast_weight(wu_ref, wub_ref)

    @pl.when(j < n_real)
    def _():
        for c in range(h_ref.shape[0] // FF_ROWS):
            rs = slice(c * FF_ROWS, (c + 1) * FF_ROWS)
            h = h_ref[rs, :]
            g = jnp.dot(h, wgb_ref[...], preferred_element_type=F32)
            u = jnp.dot(h, wub_ref[...], preferred_element_type=F32)
            o_ref[rs, :] = (g * jax.nn.sigmoid(g) * u).astype(o_ref.dtype)
        wdb_ref[...] = wd_ref[...].astype(wdb_ref.dtype)

    @pl.when(j >= n_real)
    def _():
        o_ref[...] = jnp.zeros_like(o_ref)
        wdb_ref[...] = jnp.zeros_like(wdb_ref)


FF_ROWS = 512


def _gate_up(h2, w_gate, w_up, w_down):
    tm, tn = 2048, FF_TN
    n_real = D_FF // tn
    ni = N_TOK // tm
    wd_rows = D_FF // (n_real * ni)
    assert wd_rows * n_real * ni == D_FF and wd_rows % 16 == 0
    assert (D_FF_PAD - D_FF) == (D_FF_PAD // tn - n_real) * ni * wd_rows
    wmap = lambda j, i: (0, jnp.minimum(j, n_real - 1))
    return pl.pallas_call(
        functools.partial(_gate_up_kernel, n_real=n_real),
        grid=(D_FF_PAD // tn, ni),
        in_specs=[pl.BlockSpec((tm, D_MODEL), lambda j, i: (i, 0)),
                  pl.BlockSpec((D_MODEL, tn), wmap),
                  pl.BlockSpec((D_MODEL, tn), wmap),
                  pl.BlockSpec((wd_rows, D_MODEL),
                               lambda j, i: (jnp.minimum(j * ni + i, n_real * ni - 1), 0))],
        out_specs=[pl.BlockSpec((tm, tn), lambda j, i: (i, j)),
                   pl.BlockSpec((wd_rows, D_MODEL), lambda j, i: (j * ni + i, 0))],
        out_shape=[jax.ShapeDtypeStruct((N_TOK, D_FF_PAD), BF16),
                   jax.ShapeDtypeStruct((D_FF_PAD, D_MODEL), BF16)],
        scratch_shapes=[pltpu.VMEM((D_MODEL, tn), BF16), pltpu.VMEM((D_MODEL, tn), BF16)],
        compiler_params=_params(2),
        name="ffn_gate_up",
    )(h2, w_gate, w_up, w_down)


def _down_kernel(a_ref, w_ref, x_ref, g_ref, fg_ref, o_ref, r_ref, *, nk):
    k = pl.program_id(1)

    def accumulate(first):
        for n in range(D_MODEL // DOWN_TN):
            ns = slice(n * DOWN_TN, (n + 1) * DOWN_TN)
            part = jnp.dot(a_ref[...], w_ref[:, ns], preferred_element_type=F32)
            if first:
                o_ref[:, ns] = part
            else:
                o_ref[:, ns] += part

    @pl.when(k == 0)
    def _():
        accumulate(True)

    @pl.when(k > 0)
    def _():
        accumulate(False)

    @pl.when(k == nk - 1)
    def _():
        lanes = 128

        def residual_stats(c, carry):
            rows = pl.ds(pl.multiple_of(c * APPLY_ROWS, APPLY_ROWS), APPLY_ROWS)
            ss = jnp.zeros((APPLY_ROWS, lanes), F32)
            for j in range(D_MODEL // APPLY_COLS):
                cols = slice(j * APPLY_COLS, (j + 1) * APPLY_COLS)
                x2 = x_ref[rows, cols] + g_ref[0, :, cols] * o_ref[rows, cols]
                o_ref[rows, cols] = x2
                sq = x2 * x2
                for q in range(APPLY_COLS // lanes):
                    ss = ss + sq[:, q * lanes:(q + 1) * lanes]
            ms = jnp.sum(ss, axis=-1, keepdims=True) * (1.0 / D_MODEL)
            r_ref[rows, :] = lax.rsqrt(ms + EPS)
            return carry
        lax.fori_loop(0, o_ref.shape[0] // APPLY_ROWS, residual_stats, 0)

        def normalise(rows, cols):
            o_ref[rows, cols] = o_ref[rows, cols] * r_ref[rows, :] * fg_ref[:, cols]
        _row_col_blocks(o_ref.shape[0], o_ref.shape[1], normalise)


DOWN_TN = 512


def _down(a, wd, x1, mod3, final_g, *, row0, name):
    tm, tk = 512, 1024
    nk = D_FF_PAD // tk
    n_rows = x1.shape[0]
    blk0 = row0 // tm
    return pl.pallas_call(
        functools.partial(_down_kernel, nk=nk),
        grid=(n_rows // tm, nk),
        in_specs=[pl.BlockSpec((tm, tk), lambda i, k: (blk0 + i, k)),
                  pl.BlockSpec((tk, D_MODEL), lambda i, k: (k, 0)),
                  pl.BlockSpec((tm, D_MODEL), lambda i, k: (i, 0)),
                  pl.BlockSpec((1, 1, D_MODEL), lambda i, k: (_mod_row(blk0 + i, tm) * 6 + 5, 0, 0)),
                  pl.BlockSpec((1, D_MODEL), lambda i, k: (0, 0))],
        out_specs=pl.BlockSpec((tm, D_MODEL), lambda i, k: (i, 0)),
        out_shape=jax.ShapeDtypeStruct((n_rows, D_MODEL), F32),
        scratch_shapes=[pltpu.VMEM((tm, 1), F32)],
        compiler_params=_params(2),
        name=name,
    )(a, wd, x1, mod3, final_g)


def kernel(x_prompt, x_sample, cache_k, cache_v, c, c_ctx, w_ada, b_ada, norm1_g, w_in, rpb,
           w_out, norm2_g, w_gate, w_up, w_down, final_g):
    xp = x_prompt.reshape(NP_TOK, D_MODEL)
    xs = x_sample.reshape(NS_TOK, D_MODEL)

    cvec = jnp.concatenate([c_ctx[None, :], c, jnp.zeros((N_MOD - 1 - DEC_BATCH, D_MODEL), F32)], axis=0)
    mod = _ada(cvec, w_ada[0], b_ada[0][None, :])
    mod3 = mod.reshape(N_MOD * 6, 1, D_MODEL)

    h = _norm_modulate(xp, xs, norm1_g[0][None, :], mod3, shift_part=0, scale_part=1,
                       name="norm1_mod")

    w_in0 = w_in[0]
    tm, tn = 1024, 512
    n_att_blks = D_ATT // tn
    tm2 = 2 * tm
    qu = _ws_matmul(h, w_in0, row_blk0=0, n_row_blks=N_TOK // tm2,
                    col_map=lambda j: jnp.where(j < n_att_blks, j, j + 2 * n_att_blks),
                    n_col_blks=2 * n_att_blks, tm=tm2, tn=tn, out_dtype=BF16, name="w_in_q_u",
                    vmem=VMEM_LIMIT_BIG)
    kv_s = _ws_matmul(h, w_in0, row_blk0=NP_TOK // tm2, n_row_blks=NS_TOK // tm2,
                      col_map=lambda j: j + n_att_blks, n_col_blks=2 * n_att_blks,
                      tm=tm2, tn=tn, out_dtype=BF16, name="w_in_kv_sample", vmem=VMEM_LIMIT_BIG)
    newk = _ws_matmul(h, w_in0, row_blk0=0, n_row_blks=NP_TOK // tm,
                      col_map=lambda j: j + n_att_blks, n_col_blks=n_att_blks,
                      tm=tm, tn=tn, out_dtype=F32, name="w_in_k_prompt", tiled_out=True)
    newv = _ws_matmul(h, w_in0, row_blk0=0, n_row_blks=NP_TOK // tm,
                      col_map=lambda j: j + 2 * n_att_blks, n_col_blks=n_att_blks,
                      tm=tm, tn=tn, out_dtype=F32, name="w_in_v_prompt", tiled_out=True)

    cls_of_g, patterns = _nbr_classes()
    bias = _rpb_bias(rpb[0].reshape(-1), patterns)
    ck = cache_k[:, 0].reshape(DEC_BATCH * PAST_LEN, D_ATT)
    cv = cache_v[:, 0].reshape(DEC_BATCH * PAST_LEN, D_ATT)
    att_s, w_out_b = _nbr_attention(qu, kv_s, ck, cv, bias, cls_of_g, w_out[0])
    att_p = _ctx_attention(qu, newk, newv)

    cc, sc = _dft_tables(FGROUP_DIM)
    w1 = jnp.asarray(np.concatenate([cc, -sc], axis=1)).astype(BF16)
    ab = _dft_chan(qu, w1)
    ctp, stp = _dft_tables(SEQ)
    fou_p = _dft_pos_prompt(ab, jnp.asarray(ctp).astype(BF16), jnp.asarray(stp).astype(BF16))
    cts, sts = _dft_tables(DEC_SEQ)
    fou_s = _dft_pos_sample(ab, jnp.asarray(cts).astype(BF16), jnp.asarray(sts).astype(BF16))

    x1p, k_heads = _wout(att_p, fou_p, w_out_b, xp, mod3, newk, row0=0, name="w_out_prompt")
    x1s, v_heads = _wout(att_s, fou_s, w_out_b, xs, mod3, newv, row0=NP_TOK, name="w_out_sample")

    h2 = _norm_modulate(x1p, x1s, norm2_g[0][None, :], mod3, shift_part=3, scale_part=4,
                        name="norm2_mod")
    a, wd = _gate_up(h2, w_gate[0], w_up[0], w_down[0])
    fg = final_g[None, :]
    y_prompt = _down(a, wd, x1p, mod3, fg, row0=0,
                     name="ffn_down_prompt").reshape(BATCH, SEQ, D_MODEL)
    y_sample = _down(a, wd, x1s, mod3, fg, row0=NP_TOK,
                     name="ffn_down_sample").reshape(DEC_BATCH, DEC_SEQ, D_MODEL)
    new_cache_k = k_heads.reshape(BATCH, 1, SEQ, N_HEADS, HEAD_DIM)
    new_cache_v = v_heads.reshape(BATCH, 1, SEQ, N_HEADS, HEAD_DIM)
    return (y_prompt, y_sample, new_cache_k, new_cache_v)
```

```python
import functools

import numpy as np
import jax
import jax.numpy as jnp
from jax import lax
from jax.experimental import pallas as pl
from jax.experimental.pallas import tpu as pltpu

F32 = jnp.float32
BF16 = jnp.bfloat16

D_MODEL = 4096
BATCH = 32
SEQ = 256
DEC_BATCH = 4
DEC_SEQ = 2048
PAST_LEN = 256
GRID_W = 64
GRID_ROWS = DEC_SEQ // GRID_W
D_ATT = 2048
D_FOURIER = 2048
HEAD_DIM = 128
N_HEADS = 16
N_FGROUPS = 4
FGROUP_DIM = 512
WIN_H = 8
WIN_W = 16
D_FF = 11008
D_FF_PAD = 11264
EPS = 1e-6
NEG_INF = -1e30
SCALE = HEAD_DIM ** -0.5
LOG2E = 1.4426950408889634
SCALE2 = SCALE * LOG2E

NP_TOK = BATCH * SEQ
NS_TOK = DEC_BATCH * DEC_SEQ
N_TOK = NP_TOK + NS_TOK
N_MOD = 8

VMEM_LIMIT = 56 * 1024 * 1024
VMEM_LIMIT_BIG = 60 * 1024 * 1024


def _params(n_axes, vmem=VMEM_LIMIT):
    return pltpu.CompilerParams(dimension_semantics=("arbitrary",) * n_axes,
                                vmem_limit_bytes=vmem)


def _mod_row(i, tm):
    nh = NP_TOK // tm
    return jnp.where(i < nh, 0, 1 + (i - nh) // (DEC_SEQ // tm))


def _ada_kernel(c_ref, w_ref, b_ref, o_ref):
    c = c_ref[...]
    s = (c * jax.nn.sigmoid(c)).astype(BF16)
    o_ref[...] = jnp.dot(s, w_ref[...].astype(BF16), preferred_element_type=F32) + b_ref[...]


def _ada(cvec, w_ada, b_ada):
    tn = 512
    n = w_ada.shape[1]
    return pl.pallas_call(
        _ada_kernel,
        grid=(n // tn,),
        in_specs=[pl.BlockSpec((N_MOD, D_MODEL), lambda j: (0, 0)),
                  pl.BlockSpec((D_MODEL, tn), lambda j: (0, j)),
                  pl.BlockSpec((1, tn), lambda j: (0, j))],
        out_specs=pl.BlockSpec((N_MOD, tn), lambda j: (0, j)),
        out_shape=jax.ShapeDtypeStruct((N_MOD, n), F32),
        compiler_params=_params(1),
        name="ada_mod",
    )(cvec, w_ada, b_ada)


ROW_CHUNK = 8


STATS_UNROLL = 8
APPLY_ROWS = 64
APPLY_COLS = 512


def _row_chunks(n_rows, fn):
    def body(c, carry):
        fn(pl.ds(pl.multiple_of(c * ROW_CHUNK, ROW_CHUNK), ROW_CHUNK))
        return carry
    lax.fori_loop(0, n_rows // ROW_CHUNK, body, 0, unroll=STATS_UNROLL)


def _row_col_blocks(n_rows, n_cols, fn):
    def body(c, carry):
        rows = pl.ds(pl.multiple_of(c * APPLY_ROWS, APPLY_ROWS), APPLY_ROWS)
        for j in range(n_cols // APPLY_COLS):
            fn(rows, slice(j * APPLY_COLS, (j + 1) * APPLY_COLS))
        return carry
    lax.fori_loop(0, n_rows // APPLY_ROWS, body, 0)


def _norm_mod(x_ref, g_ref, sh_ref, sc_ref, o_ref, r_ref, gs_ref):
    gs_ref[...] = g_ref[...] * (1.0 + sc_ref[0])

    def stats(rows):
        x = x_ref[rows, :]
        r_ref[rows, :] = lax.rsqrt(jnp.mean(x * x, axis=-1, keepdims=True) + EPS)
    _row_chunks(o_ref.shape[0], stats)

    def apply(rows, cols):
        o_ref[rows, cols] = (x_ref[rows, cols] * r_ref[rows, :] * gs_ref[:, cols]
                             + sh_ref[0, :, cols]).astype(o_ref.dtype)
    _row_col_blocks(o_ref.shape[0], o_ref.shape[1], apply)


def _norm_mod2_kernel(xp_ref, xs_ref, g_ref, sh_ref, sc_ref, o_ref, r_ref, gs_ref, *, nh):
    i = pl.program_id(0)

    @pl.when(i < nh)
    def _():
        _norm_mod(xp_ref, g_ref, sh_ref, sc_ref, o_ref, r_ref, gs_ref)

    @pl.when(i >= nh)
    def _():
        _norm_mod(xs_ref, g_ref, sh_ref, sc_ref, o_ref, r_ref, gs_ref)


def _mod_spec(part, tm):
    return pl.BlockSpec((1, 1, D_MODEL), lambda i: (_mod_row(i, tm) * 6 + part, 0, 0))


def _norm_modulate(xp, xs, g, mod3, *, shift_part, scale_part, name):
    tm = 512
    nh = NP_TOK // tm
    return pl.pallas_call(
        functools.partial(_norm_mod2_kernel, nh=nh),
        grid=(N_TOK // tm,),
        in_specs=[pl.BlockSpec((tm, D_MODEL), lambda i: (jnp.minimum(i, nh - 1), 0)),
                  pl.BlockSpec((tm, D_MODEL), lambda i: (jnp.maximum(i - nh, 0), 0)),
                  pl.BlockSpec((1, D_MODEL), lambda i: (0, 0)),
                  _mod_spec(shift_part, tm), _mod_spec(scale_part, tm)],
        out_specs=pl.BlockSpec((tm, D_MODEL), lambda i: (i, 0)),
        out_shape=jax.ShapeDtypeStruct((N_TOK, D_MODEL), BF16),
        scratch_shapes=[pltpu.VMEM((tm, 1), F32), pltpu.VMEM((1, D_MODEL), F32)],
        compiler_params=_params(1),
        name=name,
    )(xp, xs, g, mod3, mod3)


CAST_ROWS = 512


def _cast_weight(w_ref, wb_ref):
    rows = min(CAST_ROWS, w_ref.shape[0])

    def body(c, carry):
        r = pl.multiple_of(c * rows, rows)
        wb_ref[pl.ds(r, rows), :] = w_ref[pl.ds(r, rows), :].astype(BF16)
        return carry
    lax.fori_loop(0, w_ref.shape[0] // rows, body, 0)


WS_ROWS = 512


def _ws_kernel(x_ref, w_ref, o_ref, wb_ref):
    @pl.when(pl.program_id(1) == 0)
    def _():
        _cast_weight(w_ref, wb_ref)

    for c in range(x_ref.shape[0] // WS_ROWS):
        rs = slice(c * WS_ROWS, (c + 1) * WS_ROWS)
        res = jnp.dot(x_ref[rs, :], wb_ref[...], preferred_element_type=F32).astype(o_ref.dtype)
        if len(o_ref.shape) == 2:
            o_ref[rs, :] = res
        else:
            ts = slice(c * WS_ROWS // SUBLANES, (c + 1) * WS_ROWS // SUBLANES)
            for t in range(o_ref.shape[1]):
                o_ref[ts, t] = res[:, t * LANES:(t + 1) * LANES].reshape(WS_ROWS // SUBLANES, SUBLANES, LANES)


SUBLANES, LANES = 8, 128


def _ws_matmul(x, w, *, row_blk0, n_row_blks, col_map, n_col_blks, tm, tn, out_dtype, name,
               vmem=VMEM_LIMIT, tiled_out=False):
    k = x.shape[1]
    if tiled_out:
        out_spec = pl.BlockSpec((tm // SUBLANES, tn // LANES, SUBLANES, LANES), lambda j, i: (i, j, 0, 0))
        out_shape = (n_row_blks * tm // SUBLANES, n_col_blks * tn // LANES, SUBLANES, LANES)
    else:
        out_spec = pl.BlockSpec((tm, tn), lambda j, i: (i, j))
        out_shape = (n_row_blks * tm, n_col_blks * tn)
    return pl.pallas_call(
        _ws_kernel,
        grid=(n_col_blks, n_row_blks),
        in_specs=[pl.BlockSpec((tm, k), lambda j, i: (row_blk0 + i, 0)),
                  pl.BlockSpec((k, tn), lambda j, i: (0, col_map(j)))],
        out_specs=out_spec,
        out_shape=jax.ShapeDtypeStruct(out_shape, out_dtype),
        scratch_shapes=[pltpu.VMEM((k, tn), BF16)],
        compiler_params=_params(2, vmem),
        name=name,
    )(x, w)


def _ctx_attn_kernel(q_ref, k_ref, v_ref, o_ref):
    for h in range(N_HEADS):
        hs = slice(h * HEAD_DIM, (h + 1) * HEAD_DIM)
        q = q_ref[:, hs]
        k = k_ref[:, h].reshape(SEQ, HEAD_DIM).astype(BF16)
        v = v_ref[:, h].reshape(SEQ, HEAD_DIM).astype(BF16)
        s = lax.dot_general(q, k, (((1,), (1,)), ((), ())), preferred_element_type=F32) * SCALE2
        m = jnp.max(s, axis=-1, keepdims=True)
        p = jnp.exp2(s - m)
        l = jnp.sum(p, axis=-1, keepdims=True)
        o = jnp.dot(p.astype(BF16), v, preferred_element_type=F32) / l
        o_ref[:, hs] = o.astype(o_ref.dtype)


def _ctx_attention(qu, newk, newv):
    return pl.pallas_call(
        _ctx_attn_kernel,
        grid=(BATCH,),
        in_specs=[pl.BlockSpec((SEQ, D_ATT), lambda b: (b, 0)),
                  pl.BlockSpec((SEQ // SUBLANES, N_HEADS, SUBLANES, LANES), lambda b: (b, 0, 0, 0)),
                  pl.BlockSpec((SEQ // SUBLANES, N_HEADS, SUBLANES, LANES), lambda b: (b, 0, 0, 0))],
        out_specs=pl.BlockSpec((SEQ, D_ATT), lambda b: (b, 0)),
        out_shape=jax.ShapeDtypeStruct((NP_TOK, D_ATT), BF16),
        compiler_params=_params(1),
        name="ctx_attention",
    )(qu, newk, newv)


NA_QROWS = 4
NA_KROWS = NA_QROWS + WIN_H
NA_Q = NA_QROWS * GRID_W
NA_K = NA_KROWS * GRID_W
NA_GROUPS = GRID_ROWS // NA_QROWS


def _nbr_window_start(g):
    lo, hi = 0, GRID_ROWS - NA_KROWS
    if isinstance(g, int):
        return min(max(NA_QROWS * g - WIN_H // 2, lo), hi)
    return jnp.clip(NA_QROWS * g - WIN_H // 2, lo, hi)


def _nbr_classes():
    patterns, cls_of_g = [], []
    for g in range(NA_GROUPS):
        start = _nbr_window_start(g)
        pat = []
        for i in range(NA_QROWS):
            r = NA_QROWS * g + i
            rstart = min(max(r - WIN_H // 2, 0), GRID_ROWS - WIN_H)
            pat.append(tuple((start + j - r + WIN_H - 1) if rstart <= start + j < rstart + WIN_H else None
                             for j in range(NA_KROWS)))
        pat = tuple(pat)
        if pat not in patterns:
            patterns.append(pat)
        cls_of_g.append(patterns.index(pat))
    return tuple(cls_of_g), tuple(patterns)


def _rpb_bias_kernel(rpb_ref, o_ref, t_ref, *, patterns):
    h = pl.program_id(0)
    qc = lax.broadcasted_iota(jnp.int32, (GRID_W, GRID_W), 0)
    kc = lax.broadcasted_iota(jnp.int32, (GRID_W, GRID_W), 1)
    dc = jnp.clip(kc - qc + (WIN_W - 1), 0, 2 * WIN_W - 2)
    cstart = jnp.clip(qc - WIN_W // 2, 0, GRID_W - WIN_W)
    mask = (kc >= cstart) & (kc < cstart + WIN_W)
    n_dc = 2 * WIN_W - 1
    n_dr = 2 * WIN_H - 1
    for dr in range(n_dr):
        t = jnp.zeros((GRID_W, GRID_W), F32)
        for d in range(n_dc):
            t = jnp.where(dc == d, rpb_ref[h * (n_dr * n_dc) + dr * n_dc + d], t)
        t_ref[dr] = jnp.where(mask, t * LOG2E, NEG_INF)
    outside = jnp.full((GRID_W, GRID_W), NEG_INF, F32)
    for c, pat in enumerate(patterns):
        for i in range(NA_QROWS):
            for j in range(NA_KROWS):
                dr = pat[i][j]
                o_ref[c, 0, i * GRID_W:(i + 1) * GRID_W, j * GRID_W:(j + 1) * GRID_W] = (
                    outside if dr is None else t_ref[dr])


def _rpb_bias(rpb_flat, patterns):
    n_cls = len(patterns)
    return pl.pallas_call(
        functools.partial(_rpb_bias_kernel, patterns=patterns),
        grid=(N_HEADS,),
        in_specs=[pl.BlockSpec(memory_space=pltpu.SMEM)],
        out_specs=pl.BlockSpec((n_cls, 1, NA_Q, NA_K), lambda h: (0, h, 0, 0)),
        out_shape=jax.ShapeDtypeStruct((n_cls, N_HEADS, NA_Q, NA_K), F32),
        scratch_shapes=[pltpu.VMEM((2 * WIN_H - 1, GRID_W, GRID_W), F32)],
        compiler_params=_params(1),
        name="rpb_bias",
    )(rpb_flat)


NA_HEADS = 4
NA_COLS = NA_HEADS * HEAD_DIM


def _nbr_attn_kernel(q_ref, k_ref, v_ref, kc_ref, vc_ref, bias_ref, w_ref, o_ref, wb_ref,
                     kcb_ref, vcb_ref, *, cls_of_g):
    head0 = pl.program_id(0) * NA_HEADS
    for h in range(NA_HEADS):
        hs = slice(h * HEAD_DIM, (h + 1) * HEAD_DIM)
        rows = pl.ds(head0 + h, PAST_LEN, stride=N_HEADS)
        kcb_ref[:, hs] = kc_ref[rows, :].astype(BF16)
        vcb_ref[:, hs] = vc_ref[rows, :].astype(BF16)
    _cast_weight(w_ref, wb_ref)
    dn = (((1,), (1,)), ((), ()))

    def group_body(g, carry):
        cls = jnp.int32(cls_of_g[0])
        for gg in range(1, NA_GROUPS):
            if cls_of_g[gg] != cls_of_g[gg - 1]:
                cls = jnp.where(g >= gg, cls_of_g[gg], cls)
        q0 = pl.multiple_of(g * NA_Q, NA_Q)
        k0 = pl.multiple_of(_nbr_window_start(g) * GRID_W, GRID_W)
        for h in range(NA_HEADS):
            hs = slice(h * HEAD_DIM, (h + 1) * HEAD_DIM)
            q = q_ref[pl.ds(q0, NA_Q), hs]
            kw = k_ref[pl.ds(k0, NA_K), hs]
            vw = v_ref[pl.ds(k0, NA_K), hs]
            s_loc = lax.dot_general(q, kw, dn, preferred_element_type=F32) * SCALE2 + bias_ref[cls, h]
            s_ctx = lax.dot_general(q, kcb_ref[:, hs], dn, preferred_element_type=F32) * SCALE2
            m = jnp.maximum(jnp.max(s_loc, axis=-1, keepdims=True),
                            jnp.max(s_ctx, axis=-1, keepdims=True))
            p_loc = jnp.exp2(s_loc - m)
            p_ctx = jnp.exp2(s_ctx - m)
            l = jnp.sum(p_loc, axis=-1, keepdims=True) + jnp.sum(p_ctx, axis=-1, keepdims=True)
            o = (jnp.dot(p_loc.astype(BF16), vw, preferred_element_type=F32)
                 + jnp.dot(p_ctx.astype(BF16), vcb_ref[:, hs], preferred_element_type=F32)) / l
            o_ref[pl.ds(q0, NA_Q), hs] = o.astype(o_ref.dtype)
        return carry

    lax.fori_loop(0, NA_GROUPS, group_body, 0)


def _nbr_attention(qu, kv_s, ck, cv, bias, cls_of_g, w_out):
    n_hg = N_HEADS // NA_HEADS
    n_cls = bias.shape[0]
    row_blk0 = NP_TOK // DEC_SEQ
    slab = w_out.shape[0] // (n_hg * DEC_BATCH)
    assert slab * n_hg * DEC_BATCH == w_out.shape[0] and slab % 16 == 0
    return pl.pallas_call(
        functools.partial(_nbr_attn_kernel, cls_of_g=cls_of_g),
        grid=(n_hg, DEC_BATCH),
        in_specs=[pl.BlockSpec((DEC_SEQ, NA_COLS), lambda g, b: (row_blk0 + b, g)),
                  pl.BlockSpec((DEC_SEQ, NA_COLS), lambda g, b: (b, g)),
                  pl.BlockSpec((DEC_SEQ, NA_COLS), lambda g, b: (b, n_hg + g)),
                  pl.BlockSpec((PAST_LEN * N_HEADS, HEAD_DIM), lambda g, b: (b, 0)),
                  pl.BlockSpec((PAST_LEN * N_HEADS, HEAD_DIM), lambda g, b: (b, 0)),
                  pl.BlockSpec((n_cls, NA_HEADS, NA_Q, NA_K), lambda g, b: (0, g, 0, 0)),
                  pl.BlockSpec((slab, w_out.shape[1]), lambda g, b: (g * DEC_BATCH + b, 0))],
        out_specs=[pl.BlockSpec((DEC_SEQ, NA_COLS), lambda g, b: (b, g)),
                   pl.BlockSpec((slab, w_out.shape[1]), lambda g, b: (g * DEC_BATCH + b, 0))],
        out_shape=[jax.ShapeDtypeStruct((NS_TOK, D_ATT), BF16),
                   jax.ShapeDtypeStruct(w_out.shape, BF16)],
        scratch_shapes=[pltpu.VMEM((PAST_LEN, NA_COLS), BF16),
                        pltpu.VMEM((PAST_LEN, NA_COLS), BF16)],
        compiler_params=_params(2, VMEM_LIMIT_BIG),
        name="nbr_attention",
    )(qu, kv_s, kv_s, ck, cv, bias, w_out)


def _dft_tables(n):
    idx = np.arange(n, dtype=np.int64)
    ang = (2.0 * np.pi / n) * ((idx[:, None] * idx[None, :]) % n).astype(np.float64)
    return ((np.cos(ang) / np.sqrt(n)).astype(np.float32),
            (-np.sin(ang) / np.sqrt(n)).astype(np.float32))


def _dft_chan_kernel(u_ref, w_ref, o_ref):
    for g in range(N_FGROUPS):
        u = u_ref[:, g * FGROUP_DIM:(g + 1) * FGROUP_DIM]
        o_ref[:, g * 2 * FGROUP_DIM:(g + 1) * 2 * FGROUP_DIM] = jnp.dot(
            u, w_ref[...], preferred_element_type=F32).astype(o_ref.dtype)


def _dft_chan(qu, w1):
    tm = 1024
    return pl.pallas_call(
        _dft_chan_kernel,
        grid=(N_TOK // tm,),
        in_specs=[pl.BlockSpec((tm, D_FOURIER), lambda i: (i, 1)),
                  pl.BlockSpec((FGROUP_DIM, 2 * FGROUP_DIM), lambda i: (0, 0))],
        out_specs=pl.BlockSpec((tm, 2 * D_FOURIER), lambda i: (i, 0)),
        out_shape=jax.ShapeDtypeStruct((N_TOK, 2 * D_FOURIER), BF16),
        compiler_params=_params(1),
        name="dft_channels",
    )(qu, w1)


def _dft_pos_kernel(ct_ref, st_ref, ab_ref, o_ref):
    n_in = ct_ref.shape[1]
    n_out = ct_ref.shape[0]
    for s in range(ab_ref.shape[0] // n_in):
        for g in range(ab_ref.shape[1] // (2 * FGROUP_DIM)):
            rows = slice(s * n_in, (s + 1) * n_in)
            a = ab_ref[rows, g * 2 * FGROUP_DIM:g * 2 * FGROUP_DIM + FGROUP_DIM]
            b = ab_ref[rows, g * 2 * FGROUP_DIM + FGROUP_DIM:(g + 1) * 2 * FGROUP_DIM]
            o = (jnp.dot(ct_ref[...], a, preferred_element_type=F32)
                 + jnp.dot(st_ref[...], b, preferred_element_type=F32))
            o_ref[s * n_out:(s + 1) * n_out, g * FGROUP_DIM:(g + 1) * FGROUP_DIM] = o.astype(o_ref.dtype)


def _dft_pos_prompt(ab, ct, st):
    n_seq = 4
    rows = n_seq * SEQ
    return pl.pallas_call(
        _dft_pos_kernel,
        grid=(BATCH // n_seq,),
        in_specs=[pl.BlockSpec((SEQ, SEQ), lambda b: (0, 0)),
                  pl.BlockSpec((SEQ, SEQ), lambda b: (0, 0)),
                  pl.BlockSpec((rows, 2 * D_FOURIER), lambda b: (b, 0))],
        out_specs=pl.BlockSpec((rows, D_FOURIER), lambda b: (b, 0)),
        out_shape=jax.ShapeDtypeStruct((NP_TOK, D_FOURIER), BF16),
        compiler_params=_params(1),
        name="dft_pos_prompt",
    )(ct, st, ab)


def _dft_pos_sample(ab, ct, st):
    tr = 1024
    row_blk0 = NP_TOK // DEC_SEQ
    return pl.pallas_call(
        _dft_pos_kernel,
        grid=(DEC_BATCH, N_FGROUPS, DEC_SEQ // tr),
        in_specs=[pl.BlockSpec((tr, DEC_SEQ), lambda b, g, t: (t, 0)),
                  pl.BlockSpec((tr, DEC_SEQ), lambda b, g, t: (t, 0)),
                  pl.BlockSpec((DEC_SEQ, 2 * FGROUP_DIM), lambda b, g, t: (row_blk0 + b, g))],
        out_specs=pl.BlockSpec((tr, FGROUP_DIM), lambda b, g, t: (b * (DEC_SEQ // tr) + t, g)),
        out_shape=jax.ShapeDtypeStruct((NS_TOK, D_FOURIER), BF16),
        compiler_params=_params(3),
        name="dft_pos_sample",
    )(ct, st, ab)


def _wout_kernel(a_ref, f_ref, w_ref, x_ref, g_ref, tiled_ref, o_ref, heads_ref):
    for c in range(a_ref.shape[0] // WS_ROWS):
        rs = slice(c * WS_ROWS, (c + 1) * WS_ROWS)
        acc = (jnp.dot(a_ref[rs, :], w_ref[0:D_ATT, :], preferred_element_type=F32)
               + jnp.dot(f_ref[rs, :], w_ref[D_ATT:D_ATT + D_FOURIER, :], preferred_element_type=F32))
        o_ref[rs, :] = x_ref[rs, :] + g_ref[0] * acc

    for tt in range(tiled_ref.shape[0] // (N_HEADS * SUBLANES)):
        for s in range(SUBLANES):
            for hb in range(N_HEADS // SUBLANES):
                src = pl.ds(tt * (N_HEADS * SUBLANES) + hb * SUBLANES * SUBLANES + s, SUBLANES,
                            stride=SUBLANES)
                dst = pl.ds((tt * SUBLANES + s) * N_HEADS + hb * SUBLANES, SUBLANES)
                heads_ref[dst, :] = tiled_ref[src, :]


def _wout(att, fou, w_out, x, mod3, tiled, *, row0, name):
    assert HEAD_DIM == LANES and N_HEADS % SUBLANES == 0
    tm, tn = 1024, 1024
    n_rows = x.shape[0]
    blk0 = row0 // tm
    nj, ni = D_MODEL // tn, n_rows // tm
    flat = tiled.reshape(-1, LANES)
    slab = flat.shape[0] // (nj * ni)
    assert slab * nj * ni == flat.shape[0] and slab % (N_HEADS * SUBLANES) == 0
    return pl.pallas_call(
        _wout_kernel,
        grid=(nj, ni),
        in_specs=[pl.BlockSpec((tm, D_ATT), lambda j, i: (i, 0)),
                  pl.BlockSpec((tm, D_FOURIER), lambda j, i: (i, 0)),
                  pl.BlockSpec((D_MODEL, tn), lambda j, i: (0, j)),
                  pl.BlockSpec((tm, tn), lambda j, i: (i, j)),
                  pl.BlockSpec((1, 1, tn), lambda j, i: (_mod_row(blk0 + i, tm) * 6 + 2, 0, j)),
                  pl.BlockSpec((slab, LANES), lambda j, i: (j * ni + i, 0))],
        out_specs=[pl.BlockSpec((tm, tn), lambda j, i: (i, j)),
                   pl.BlockSpec((slab, LANES), lambda j, i: (j * ni + i, 0))],
        out_shape=[jax.ShapeDtypeStruct((n_rows, D_MODEL), F32),
                   jax.ShapeDtypeStruct(flat.shape, F32)],
        compiler_params=_params(2, VMEM_LIMIT_BIG),
        name=name,
    )(att, fou, w_out, x, mod3, flat)


FF_TN = 256


def _gate_up_kernel(h_ref, wg_ref, wu_ref, wd_ref, o_ref, wdb_ref, wgb_ref, wub_ref, *, n_real):
    j = pl.program_id(0)

    @pl.when((pl.program_id(1) == 0) & (j < n_real))
    def _():
        _cast_weight(wg_ref, wgb_ref)
        _cast_weight(wu_ref, wub_ref)

    @pl.when(j < n_real)
    def _():
        for c in range(h_ref.shape[0] // FF_ROWS):
            rs = slice(c * FF_ROWS, (c + 1) * FF_ROWS)
            h = h_ref[rs, :]
            g = jnp.dot(h, wgb_ref[...], preferred_element_type=F32)
            u = jnp.dot(h, wub_ref[...], preferred_element_type=F32)
            o_ref[rs, :] = (g * jax.nn.sigmoid(g) * u).astype(o_ref.dtype)
        wdb_ref[...] = wd_ref[...].astype(wdb_ref.dtype)

    @pl.when(j >= n_real)
    def _():
        o_ref[...] = jnp.zeros_like(o_ref)
        wdb_ref[...] = jnp.zeros_like(wdb_ref)


FF_ROWS = 512


def _gate_up(h2, w_gate, w_up, w_down):
    tm, tn = 2048, FF_TN
    n_real = D_FF // tn
    ni = N_TOK // tm
    wd_rows = D_FF // (n_real * ni)
    assert wd_rows * n_real * ni == D_FF and wd_rows % 16 == 0
    assert (D_FF_PAD - D_FF) == (D_FF_PAD // tn - n_real) * ni * wd_rows
    wmap = lambda j, i: (0, jnp.minimum(j, n_real - 1))
    return pl.pallas_call(
        functools.partial(_gate_up_kernel, n_real=n_real),
        grid=(D_FF_PAD // tn, ni),
        in_specs=[pl.BlockSpec((tm, D_MODEL), lambda j, i: (jnp.where(j < n_real, i, ni - 1), 0)),
                  pl.BlockSpec((D_MODEL, tn), wmap),
                  pl.BlockSpec((D_MODEL, tn), wmap),
                  pl.BlockSpec((wd_rows, D_MODEL),
                               lambda j, i: (jnp.minimum(j * ni + i, n_real * ni - 1), 0))],
        out_specs=[pl.BlockSpec((tm, tn), lambda j, i: (i, j)),
                   pl.BlockSpec((wd_rows, D_MODEL), lambda j, i: (j * ni + i, 0))],
        out_shape=[jax.ShapeDtypeStruct((N_TOK, D_FF_PAD), BF16),
                   jax.ShapeDtypeStruct((D_FF_PAD, D_MODEL), BF16)],
        scratch_shapes=[pltpu.VMEM((D_MODEL, tn), BF16), pltpu.VMEM((D_MODEL, tn), BF16)],
        compiler_params=_params(2),
        name="ffn_gate_up",
    )(h2, w_gate, w_up, w_down)


def _down_kernel(a_ref, w_ref, x_ref, g_ref, fg_ref, o_ref, r_ref, *, nk):
    k = pl.program_id(1)

    tk = a_ref.shape[1]
    k_last = D_FF - (nk - 1) * tk

    def accumulate(first, depth):
        for n in range(D_MODEL // DOWN_TN):
            ns = slice(n * DOWN_TN, (n + 1) * DOWN_TN)
            part = jnp.dot(a_ref[:, :depth], w_ref[:depth, ns], preferred_element_type=F32)
            if first:
                o_ref[:, ns] = part
            else:
                o_ref[:, ns] += part

    @pl.when(k == 0)
    def _():
        accumulate(True, tk)

    @pl.when((k > 0) & (k < nk - 1))
    def _():
        accumulate(False, tk)

    @pl.when(k == nk - 1)
    def _():
        accumulate(False, k_last)

    @pl.when(k == nk - 1)
    def _():
        lanes = 128

        def residual_stats(c, carry):
            rows = pl.ds(pl.multiple_of(c * APPLY_ROWS, APPLY_ROWS), APPLY_ROWS)
            ss = jnp.zeros((APPLY_ROWS, lanes), F32)
            for j in range(D_MODEL // APPLY_COLS):
                cols = slice(j * APPLY_COLS, (j + 1) * APPLY_COLS)
                x2 = x_ref[rows, cols] + g_ref[0, :, cols] * o_ref[rows, cols]
                o_ref[rows, cols] = x2
                sq = x2 * x2
                for q in range(APPLY_COLS // lanes):
                    ss = ss + sq[:, q * lanes:(q + 1) * lanes]
            ms = jnp.sum(ss, axis=-1, keepdims=True) * (1.0 / D_MODEL)
            r_ref[rows, :] = lax.rsqrt(ms + EPS)
            return carry
        lax.fori_loop(0, o_ref.shape[0] // APPLY_ROWS, residual_stats, 0)

        def normalise(rows, cols):
            o_ref[rows, cols] = o_ref[rows, cols] * r_ref[rows, :] * fg_ref[:, cols]
        _row_col_blocks(o_ref.shape[0], o_ref.shape[1], normalise)


DOWN_TN = 512


def _down(a, wd, x1, mod3, final_g, *, row0, name):
    tm, tk = 512, 1024
    nk = D_FF_PAD // tk
    assert nk * tk == D_FF_PAD
    n_rows = x1.shape[0]
    blk0 = row0 // tm
    return pl.pallas_call(
        functools.partial(_down_kernel, nk=nk),
        grid=(n_rows // tm, nk),
        in_specs=[pl.BlockSpec((tm, tk), lambda i, k: (blk0 + i, k)),
                  pl.BlockSpec((tk, D_MODEL), lambda i, k: (k, 0)),
                  pl.BlockSpec((tm, D_MODEL), lambda i, k: (i, 0)),
                  pl.BlockSpec((1, 1, D_MODEL), lambda i, k: (_mod_row(blk0 + i, tm) * 6 + 5, 0, 0)),
                  pl.BlockSpec((1, D_MODEL), lambda i, k: (0, 0))],
        out_specs=pl.BlockSpec((tm, D_MODEL), lambda i, k: (i, 0)),
        out_shape=jax.ShapeDtypeStruct((n_rows, D_MODEL), F32),
        scratch_shapes=[pltpu.VMEM((tm, 1), F32)],
        compiler_params=_params(2),
        name=name,
    )(a, wd, x1, mod3, final_g)


def kernel(x_prompt, x_sample, cache_k, cache_v, c, c_ctx, w_ada, b_ada, norm1_g, w_in, rpb,
           w_out, norm2_g, w_gate, w_up, w_down, final_g):
    xp = x_prompt.reshape(NP_TOK, D_MODEL)
    xs = x_sample.reshape(NS_TOK, D_MODEL)

    cvec = jnp.concatenate([c_ctx[None, :], c, jnp.zeros((N_MOD - 1 - DEC_BATCH, D_MODEL), F32)], axis=0)
    mod = _ada(cvec, w_ada[0], b_ada[0][None, :])
    mod3 = mod.reshape(N_MOD * 6, 1, D_MODEL)

    h = _norm_modulate(xp, xs, norm1_g[0][None, :], mod3, shift_part=0, scale_part=1,
                       name="norm1_mod")

    w_in0 = w_in[0]
    tm, tn = 1024, 512
    n_att_blks = D_ATT // tn
    tm2 = 2 * tm
    qu = _ws_matmul(h, w_in0, row_blk0=0, n_row_blks=N_TOK // tm2,
                    col_map=lambda j: jnp.where(j < n_att_blks, j, j + 2 * n_att_blks),
                    n_col_blks=2 * n_att_blks, tm=tm2, tn=tn, out_dtype=BF16, name="w_in_q_u",
                    vmem=VMEM_LIMIT_BIG)
    kv_s = _ws_matmul(h, w_in0, row_blk0=NP_TOK // tm2, n_row_blks=NS_TOK // tm2,
                      col_map=lambda j: j + n_att_blks, n_col_blks=2 * n_att_blks,
                      tm=tm2, tn=tn, out_dtype=BF16, name="w_in_kv_sample", vmem=VMEM_LIMIT_BIG)
    newk = _ws_matmul(h, w_in0, row_blk0=0, n_row_blks=NP_TOK // tm,
                      col_map=lambda j: j + n_att_blks, n_col_blks=n_att_blks,
                      tm=tm, tn=tn, out_dtype=F32, name="w_in_k_prompt", tiled_out=True)
    newv = _ws_matmul(h, w_in0, row_blk0=0, n_row_blks=NP_TOK // tm,
                      col_map=lambda j: j + 2 * n_att_blks, n_col_blks=n_att_blks,
                      tm=tm, tn=tn, out_dtype=F32, name="w_in_v_prompt", tiled_out=True)

    cls_of_g, patterns = _nbr_classes()
    bias = _rpb_bias(rpb[0].reshape(-1), patterns)
    ck = cache_k[:, 0].reshape(DEC_BATCH * PAST_LEN * N_HEADS, HEAD_DIM)
    cv = cache_v[:, 0].reshape(DEC_BATCH * PAST_LEN * N_HEADS, HEAD_DIM)
    att_s, w_out_b = _nbr_attention(qu, kv_s, ck, cv, bias, cls_of_g, w_out[0])
    att_p = _ctx_attention(qu, newk, newv)

    cc, sc = _dft_tables(FGROUP_DIM)
    w1 = jnp.asarray(np.concatenate([cc, -sc], axis=1)).astype(BF16)
    ab = _dft_chan(qu, w1)
    ctp, stp = _dft_tables(SEQ)
    fou_p = _dft_pos_prompt(ab, jnp.asarray(ctp).astype(BF16), jnp.asarray(stp).astype(BF16))
    cts, sts = _dft_tables(DEC_SEQ)
    fou_s = _dft_pos_sample(ab, jnp.asarray(cts).astype(BF16), jnp.asarray(sts).astype(BF16))

    x1p, k_heads = _wout(att_p, fou_p, w_out_b, xp, mod3, newk, row0=0, name="w_out_prompt")
    x1s, v_heads = _wout(att_s, fou_s, w_out_b, xs, mod3, newv, row0=NP_TOK, name="w_out_sample")

    h2 = _norm_modulate(x1p, x1s, norm2_g[0][None, :], mod3, shift_part=3, scale_part=4,
                        name="norm2_mod")
    a, wd = _gate_up(h2, w_gate[0], w_up[0], w_down[0])
    fg = final_g[None, :]
    y_prompt = _down(a, wd, x1p, mod3, fg, row0=0,
                     name="ffn_down_prompt").reshape(BATCH, SEQ, D_MODEL)
    y_sample = _down(a, wd, x1s, mod3, fg, row0=NP_TOK,
                     name="ffn_down_sample").reshape(DEC_BATCH, DEC_SEQ, D_MODEL)
    new_cache_k = k_heads.reshape(BATCH, 1, SEQ, N_HEADS, HEAD_DIM)
    new_cache_v = v_heads.reshape(BATCH, 1, SEQ, N_HEADS, HEAD_DIM)
    return (y_prompt, y_sample, new_cache_k, new_cache_v)
```

```python
import functools

import numpy as np
import jax
import jax.numpy as jnp
from jax import lax
from jax.experimental import pallas as pl
from jax.experimental.pallas import tpu as pltpu

F32 = jnp.float32
BF16 = jnp.bfloat16

D_MODEL = 4096
BATCH = 32
SEQ = 256
DEC_BATCH = 4
DEC_SEQ = 2048
PAST_LEN = 256
GRID_W = 64
GRID_ROWS = DEC_SEQ // GRID_W
D_ATT = 2048
D_FOURIER = 2048
HEAD_DIM = 128
N_HEADS = 16
N_FGROUPS = 4
FGROUP_DIM = 512
WIN_H = 8
WIN_W = 16
D_FF = 11008
D_FF_PAD = 11264
EPS = 1e-6
NEG_INF = -1e30
SCALE = HEAD_DIM ** -0.5
LOG2E = 1.4426950408889634
SCALE2 = SCALE * LOG2E

NP_TOK = BATCH * SEQ
NS_TOK = DEC_BATCH * DEC_SEQ
N_TOK = NP_TOK + NS_TOK
N_MOD = 8

VMEM_LIMIT = 56 * 1024 * 1024
VMEM_LIMIT_BIG = 60 * 1024 * 1024


def _params(n_axes, vmem=VMEM_LIMIT):
    return pltpu.CompilerParams(dimension_semantics=("arbitrary",) * n_axes,
                                vmem_limit_bytes=vmem)


def _mod_row(i, tm):
    nh = NP_TOK // tm
    return jnp.where(i < nh, 0, 1 + (i - nh) // (DEC_SEQ // tm))


def _ada_kernel(c_ref, w_ref, b_ref, o_ref):
    c = c_ref[...]
    s = (c * jax.nn.sigmoid(c)).astype(BF16)
    o_ref[...] = jnp.dot(s, w_ref[...].astype(BF16), preferred_element_type=F32) + b_ref[...]


def _ada(cvec, w_ada, b_ada):
    tn = 1024
    n = w_ada.shape[1]
    return pl.pallas_call(
        _ada_kernel,
        grid=(n // tn,),
        in_specs=[pl.BlockSpec((N_MOD, D_MODEL), lambda j: (0, 0)),
                  pl.BlockSpec((D_MODEL, tn), lambda j: (0, j)),
                  pl.BlockSpec((1, tn), lambda j: (0, j))],
        out_specs=pl.BlockSpec((N_MOD, tn), lambda j: (0, j)),
        out_shape=jax.ShapeDtypeStruct((N_MOD, n), F32),
        compiler_params=_params(1),
        name="ada_mod",
    )(cvec, w_ada, b_ada)


ROW_CHUNK = 8


STATS_UNROLL = 16
APPLY_ROWS = 64
APPLY_COLS = 512


def _row_chunks(n_rows, fn):
    def body(c, carry):
        fn(pl.ds(pl.multiple_of(c * ROW_CHUNK, ROW_CHUNK), ROW_CHUNK))
        return carry
    lax.fori_loop(0, n_rows // ROW_CHUNK, body, 0, unroll=STATS_UNROLL)


def _row_col_blocks(n_rows, n_cols, fn):
    def body(c, carry):
        rows = pl.ds(pl.multiple_of(c * APPLY_ROWS, APPLY_ROWS), APPLY_ROWS)
        for j in range(n_cols // APPLY_COLS):
            fn(rows, slice(j * APPLY_COLS, (j + 1) * APPLY_COLS))
        return carry
    lax.fori_loop(0, n_rows // APPLY_ROWS, body, 0)


def _norm_mod(x_ref, g_ref, sh_ref, sc_ref, o_ref, r_ref, gs_ref):
    gs_ref[...] = g_ref[...] * (1.0 + sc_ref[0])

    def stats(rows):
        x = x_ref[rows, :]
        r_ref[rows, :] = lax.rsqrt(jnp.mean(x * x, axis=-1, keepdims=True) + EPS)
    _row_chunks(o_ref.shape[0], stats)

    def apply(rows, cols):
        o_ref[rows, cols] = (x_ref[rows, cols] * r_ref[rows, :] * gs_ref[:, cols]
                             + sh_ref[0, :, cols]).astype(o_ref.dtype)
    _row_col_blocks(o_ref.shape[0], o_ref.shape[1], apply)


def _norm_mod2_kernel(xp_ref, xs_ref, g_ref, sh_ref, sc_ref, o_ref, r_ref, gs_ref, *, nh):
    i = pl.program_id(0)

    @pl.when(i < nh)
    def _():
        _norm_mod(xp_ref, g_ref, sh_ref, sc_ref, o_ref, r_ref, gs_ref)

    @pl.when(i >= nh)
    def _():
        _norm_mod(xs_ref, g_ref, sh_ref, sc_ref, o_ref, r_ref, gs_ref)


def _mod_spec(part, tm):
    return pl.BlockSpec((1, 1, D_MODEL), lambda i: (_mod_row(i, tm) * 6 + part, 0, 0))


def _norm_modulate(xp, xs, g, mod3, *, shift_part, scale_part, name):
    tm = 512
    nh = NP_TOK // tm
    return pl.pallas_call(
        functools.partial(_norm_mod2_kernel, nh=nh),
        grid=(N_TOK // tm,),
        in_specs=[pl.BlockSpec((tm, D_MODEL), lambda i: (jnp.minimum(i, nh - 1), 0)),
                  pl.BlockSpec((tm, D_MODEL), lambda i: (jnp.maximum(i - nh, 0), 0)),
                  pl.BlockSpec((1, D_MODEL), lambda i: (0, 0)),
                  _mod_spec(shift_part, tm), _mod_spec(scale_part, tm)],
        out_specs=pl.BlockSpec((tm, D_MODEL), lambda i: (i, 0)),
        out_shape=jax.ShapeDtypeStruct((N_TOK, D_MODEL), BF16),
        scratch_shapes=[pltpu.VMEM((tm, 1), F32), pltpu.VMEM((1, D_MODEL), F32)],
        compiler_params=_params(1),
        name=name,
    )(xp, xs, g, mod3, mod3)


CAST_ROWS = 512


def _cast_weight(w_ref, wb_ref):
    rows = min(CAST_ROWS, w_ref.shape[0])

    def body(c, carry):
        r = pl.multiple_of(c * rows, rows)
        wb_ref[pl.ds(r, rows), :] = w_ref[pl.ds(r, rows), :].astype(BF16)
        return carry
    lax.fori_loop(0, w_ref.shape[0] // rows, body, 0)


WS_ROWS = 512


def _ws_kernel(x_ref, w_ref, o_ref, wb_ref):
    @pl.when(pl.program_id(1) == 0)
    def _():
        _cast_weight(w_ref, wb_ref)

    for c in range(x_ref.shape[0] // WS_ROWS):
        rs = slice(c * WS_ROWS, (c + 1) * WS_ROWS)
        res = jnp.dot(x_ref[rs, :], wb_ref[...], preferred_element_type=F32).astype(o_ref.dtype)
        if len(o_ref.shape) == 2:
            o_ref[rs, :] = res
        else:
            ts = slice(c * WS_ROWS // SUBLANES, (c + 1) * WS_ROWS // SUBLANES)
            for t in range(o_ref.shape[1]):
                o_ref[ts, t] = res[:, t * LANES:(t + 1) * LANES].reshape(WS_ROWS // SUBLANES, SUBLANES, LANES)


SUBLANES, LANES = 8, 128


def _ws_matmul(x, w, *, row_blk0, n_row_blks, col_map, n_col_blks, tm, tn, out_dtype, name,
               vmem=VMEM_LIMIT, tiled_out=False):
    k = x.shape[1]
    if tiled_out:
        out_spec = pl.BlockSpec((tm // SUBLANES, tn // LANES, SUBLANES, LANES), lambda j, i: (i, j, 0, 0))
        out_shape = (n_row_blks * tm // SUBLANES, n_col_blks * tn // LANES, SUBLANES, LANES)
    else:
        out_spec = pl.BlockSpec((tm, tn), lambda j, i: (i, j))
        out_shape = (n_row_blks * tm, n_col_blks * tn)
    return pl.pallas_call(
        _ws_kernel,
        grid=(n_col_blks, n_row_blks),
        in_specs=[pl.BlockSpec((tm, k), lambda j, i: (row_blk0 + i, 0)),
                  pl.BlockSpec((k, tn), lambda j, i: (0, col_map(j)))],
        out_specs=out_spec,
        out_shape=jax.ShapeDtypeStruct(out_shape, out_dtype),
        scratch_shapes=[pltpu.VMEM((k, tn), BF16)],
        compiler_params=_params(2, vmem),
        name=name,
    )(x, w)


def _ctx_attn_kernel(q_ref, k_ref, v_ref, o_ref):
    for h in range(N_HEADS):
        hs = slice(h * HEAD_DIM, (h + 1) * HEAD_DIM)
        q = q_ref[:, hs]
        k = k_ref[:, h].reshape(SEQ, HEAD_DIM).astype(BF16)
        v = v_ref[:, h].reshape(SEQ, HEAD_DIM).astype(BF16)
        s = lax.dot_general(q, k, (((1,), (1,)), ((), ())), preferred_element_type=F32) * SCALE2
        m = jnp.max(s, axis=-1, keepdims=True)
        p = jnp.exp2(s - m)
        l = jnp.sum(p, axis=-1, keepdims=True)
        o = jnp.dot(p.astype(BF16), v, preferred_element_type=F32) / l
        o_ref[:, hs] = o.astype(o_ref.dtype)


def _ctx_attention(qu, newk, newv):
    return pl.pallas_call(
        _ctx_attn_kernel,
        grid=(BATCH,),
        in_specs=[pl.BlockSpec((SEQ, D_ATT), lambda b: (b, 0)),
                  pl.BlockSpec((SEQ // SUBLANES, N_HEADS, SUBLANES, LANES), lambda b: (b, 0, 0, 0)),
                  pl.BlockSpec((SEQ // SUBLANES, N_HEADS, SUBLANES, LANES), lambda b: (b, 0, 0, 0))],
        out_specs=pl.BlockSpec((SEQ, D_ATT), lambda b: (b, 0)),
        out_shape=jax.ShapeDtypeStruct((NP_TOK, D_ATT), BF16),
        compiler_params=_params(1),
        name="ctx_attention",
    )(qu, newk, newv)


NA_QROWS = 4
NA_KROWS = NA_QROWS + WIN_H
NA_Q = NA_QROWS * GRID_W
NA_K = NA_KROWS * GRID_W
NA_GROUPS = GRID_ROWS // NA_QROWS


def _nbr_window_start(g):
    lo, hi = 0, GRID_ROWS - NA_KROWS
    if isinstance(g, int):
        return min(max(NA_QROWS * g - WIN_H // 2, lo), hi)
    return jnp.clip(NA_QROWS * g - WIN_H // 2, lo, hi)


def _nbr_classes():
    patterns, cls_of_g = [], []
    for g in range(NA_GROUPS):
        start = _nbr_window_start(g)
        pat = []
        for i in range(NA_QROWS):
            r = NA_QROWS * g + i
            rstart = min(max(r - WIN_H // 2, 0), GRID_ROWS - WIN_H)
            pat.append(tuple((start + j - r + WIN_H - 1) if rstart <= start + j < rstart + WIN_H else None
                             for j in range(NA_KROWS)))
        pat = tuple(pat)
        if pat not in patterns:
            patterns.append(pat)
        cls_of_g.append(patterns.index(pat))
    return tuple(cls_of_g), tuple(patterns)


def _rpb_bias_kernel(rpb_ref, o_ref, t_ref, *, patterns):
    h = pl.program_id(0)
    qc = lax.broadcasted_iota(jnp.int32, (GRID_W, GRID_W), 0)
    kc = lax.broadcasted_iota(jnp.int32, (GRID_W, GRID_W), 1)
    dc = jnp.clip(kc - qc + (WIN_W - 1), 0, 2 * WIN_W - 2)
    cstart = jnp.clip(qc - WIN_W // 2, 0, GRID_W - WIN_W)
    mask = (kc >= cstart) & (kc < cstart + WIN_W)
    n_dc = 2 * WIN_W - 1
    n_dr = 2 * WIN_H - 1
    for dr in range(n_dr):
        t = jnp.zeros((GRID_W, GRID_W), F32)
        for d in range(n_dc):
            t = jnp.where(dc == d, rpb_ref[h * (n_dr * n_dc) + dr * n_dc + d], t)
        t_ref[dr] = jnp.where(mask, t * LOG2E, NEG_INF)
    outside = jnp.full((GRID_W, GRID_W), NEG_INF, F32)
    for c, pat in enumerate(patterns):
        for i in range(NA_QROWS):
            for j in range(NA_KROWS):
                dr = pat[i][j]
                o_ref[c, 0, i * GRID_W:(i + 1) * GRID_W, j * GRID_W:(j + 1) * GRID_W] = (
                    outside if dr is None else t_ref[dr])


def _rpb_bias(rpb_flat, patterns):
    n_cls = len(patterns)
    return pl.pallas_call(
        functools.partial(_rpb_bias_kernel, patterns=patterns),
        grid=(N_HEADS,),
        in_specs=[pl.BlockSpec(memory_space=pltpu.SMEM)],
        out_specs=pl.BlockSpec((n_cls, 1, NA_Q, NA_K), lambda h: (0, h, 0, 0)),
        out_shape=jax.ShapeDtypeStruct((n_cls, N_HEADS, NA_Q, NA_K), F32),
        scratch_shapes=[pltpu.VMEM((2 * WIN_H - 1, GRID_W, GRID_W), F32)],
        compiler_params=_params(1),
        name="rpb_bias",
    )(rpb_flat)


NA_HEADS = 4
NA_COLS = NA_HEADS * HEAD_DIM


def _nbr_attn_kernel(q_ref, k_ref, v_ref, kc_ref, vc_ref, bias_ref, w_ref, o_ref, wb_ref,
                     kcb_ref, vcb_ref, *, cls_of_g):
    head0 = pl.program_id(0) * NA_HEADS
    for h in range(NA_HEADS):
        hs = slice(h * HEAD_DIM, (h + 1) * HEAD_DIM)
        rows = pl.ds(head0 + h, PAST_LEN, stride=N_HEADS)
        kcb_ref[:, hs] = kc_ref[rows, :].astype(BF16)
        vcb_ref[:, hs] = vc_ref[rows, :].astype(BF16)
    _cast_weight(w_ref, wb_ref)
    dn = (((1,), (1,)), ((), ()))

    def group_body(g, carry):
        cls = jnp.int32(cls_of_g[0])
        for gg in range(1, NA_GROUPS):
            if cls_of_g[gg] != cls_of_g[gg - 1]:
                cls = jnp.where(g >= gg, cls_of_g[gg], cls)
        q0 = pl.multiple_of(g * NA_Q, NA_Q)
        k0 = pl.multiple_of(_nbr_window_start(g) * GRID_W, GRID_W)
        for h in range(NA_HEADS):
            hs = slice(h * HEAD_DIM, (h + 1) * HEAD_DIM)
            q = q_ref[pl.ds(q0, NA_Q), hs]
            kw = k_ref[pl.ds(k0, NA_K), hs]
            vw = v_ref[pl.ds(k0, NA_K), hs]
            s_loc = lax.dot_general(q, kw, dn, preferred_element_type=F32) * SCALE2 + bias_ref[cls, h]
            s_ctx = lax.dot_general(q, kcb_ref[:, hs], dn, preferred_element_type=F32) * SCALE2
            m = jnp.maximum(jnp.max(s_loc, axis=-1, keepdims=True),
                            jnp.max(s_ctx, axis=-1, keepdims=True))
            p_loc = jnp.exp2(s_loc - m)
            p_ctx = jnp.exp2(s_ctx - m)
            l = jnp.sum(p_loc, axis=-1, keepdims=True) + jnp.sum(p_ctx, axis=-1, keepdims=True)
            o = (jnp.dot(p_loc.astype(BF16), vw, preferred_element_type=F32)
                 + jnp.dot(p_ctx.astype(BF16), vcb_ref[:, hs], preferred_element_type=F32)) / l
            o_ref[pl.ds(q0, NA_Q), hs] = o.astype(o_ref.dtype)
        return carry

    lax.fori_loop(0, NA_GROUPS, group_body, 0)


def _nbr_attention(qu, kv_s, ck, cv, bias, cls_of_g, w_out):
    n_hg = N_HEADS // NA_HEADS
    n_cls = bias.shape[0]
    row_blk0 = NP_TOK // DEC_SEQ
    slab = w_out.shape[0] // (n_hg * DEC_BATCH)
    assert slab * n_hg * DEC_BATCH == w_out.shape[0] and slab % 16 == 0
    return pl.pallas_call(
        functools.partial(_nbr_attn_kernel, cls_of_g=cls_of_g),
        grid=(n_hg, DEC_BATCH),
        in_specs=[pl.BlockSpec((DEC_SEQ, NA_COLS), lambda g, b: (row_blk0 + b, g)),
                  pl.BlockSpec((DEC_SEQ, NA_COLS), lambda g, b: (b, g)),
                  pl.BlockSpec((DEC_SEQ, NA_COLS), lambda g, b: (b, n_hg + g)),
                  pl.BlockSpec((PAST_LEN * N_HEADS, HEAD_DIM), lambda g, b: (b, 0)),
                  pl.BlockSpec((PAST_LEN * N_HEADS, HEAD_DIM), lambda g, b: (b, 0)),
                  pl.BlockSpec((n_cls, NA_HEADS, NA_Q, NA_K), lambda g, b: (0, g, 0, 0)),
                  pl.BlockSpec((slab, w_out.shape[1]), lambda g, b: (g * DEC_BATCH + b, 0))],
        out_specs=[pl.BlockSpec((DEC_SEQ, NA_COLS), lambda g, b: (b, g)),
                   pl.BlockSpec((slab, w_out.shape[1]), lambda g, b: (g * DEC_BATCH + b, 0))],
        out_shape=[jax.ShapeDtypeStruct((NS_TOK, D_ATT), BF16),
                   jax.ShapeDtypeStruct(w_out.shape, BF16)],
        scratch_shapes=[pltpu.VMEM((PAST_LEN, NA_COLS), BF16),
                        pltpu.VMEM((PAST_LEN, NA_COLS), BF16)],
        compiler_params=_params(2, VMEM_LIMIT_BIG),
        name="nbr_attention",
    )(qu, kv_s, kv_s, ck, cv, bias, w_out)


def _dft_tables(n):
    idx = np.arange(n, dtype=np.int64)
    ang = (2.0 * np.pi / n) * ((idx[:, None] * idx[None, :]) % n).astype(np.float64)
    return ((np.cos(ang) / np.sqrt(n)).astype(np.float32),
            (-np.sin(ang) / np.sqrt(n)).astype(np.float32))


def _dft_chan_kernel(u_ref, w_ref, o_ref):
    for c in range(u_ref.shape[0] // WS_ROWS):
        rs = slice(c * WS_ROWS, (c + 1) * WS_ROWS)
        for g in range(N_FGROUPS):
            u = u_ref[rs, g * FGROUP_DIM:(g + 1) * FGROUP_DIM]
            o_ref[rs, g * 2 * FGROUP_DIM:(g + 1) * 2 * FGROUP_DIM] = jnp.dot(
                u, w_ref[...], preferred_element_type=F32).astype(o_ref.dtype)


def _dft_chan(qu, w1):
    tm = 2048
    return pl.pallas_call(
        _dft_chan_kernel,
        grid=(N_TOK // tm,),
        in_specs=[pl.BlockSpec((tm, D_FOURIER), lambda i: (i, 1)),
                  pl.BlockSpec((FGROUP_DIM, 2 * FGROUP_DIM), lambda i: (0, 0))],
        out_specs=pl.BlockSpec((tm, 2 * D_FOURIER), lambda i: (i, 0)),
        out_shape=jax.ShapeDtypeStruct((N_TOK, 2 * D_FOURIER), BF16),
        compiler_params=_params(1),
        name="dft_channels",
    )(qu, w1)


def _dft_pos_kernel(ct_ref, st_ref, ab_ref, o_ref):
    n_in = ct_ref.shape[1]
    n_out = ct_ref.shape[0]
    for s in range(ab_ref.shape[0] // n_in):
        for g in range(ab_ref.shape[1] // (2 * FGROUP_DIM)):
            rows = slice(s * n_in, (s + 1) * n_in)
            a = ab_ref[rows, g * 2 * FGROUP_DIM:g * 2 * FGROUP_DIM + FGROUP_DIM]
            b = ab_ref[rows, g * 2 * FGROUP_DIM + FGROUP_DIM:(g + 1) * 2 * FGROUP_DIM]
            o = (jnp.dot(ct_ref[...], a, preferred_element_type=F32)
                 + jnp.dot(st_ref[...], b, preferred_element_type=F32))
            o_ref[s * n_out:(s + 1) * n_out, g * FGROUP_DIM:(g + 1) * FGROUP_DIM] = o.astype(o_ref.dtype)


def _dft_pos_prompt(ab, ct, st):
    n_seq = 4
    rows = n_seq * SEQ
    return pl.pallas_call(
        _dft_pos_kernel,
        grid=(BATCH // n_seq,),
        in_specs=[pl.BlockSpec((SEQ, SEQ), lambda b: (0, 0)),
                  pl.BlockSpec((SEQ, SEQ), lambda b: (0, 0)),
                  pl.BlockSpec((rows, 2 * D_FOURIER), lambda b: (b, 0))],
        out_specs=pl.BlockSpec((rows, D_FOURIER), lambda b: (b, 0)),
        out_shape=jax.ShapeDtypeStruct((NP_TOK, D_FOURIER), BF16),
        compiler_params=_params(1),
        name="dft_pos_prompt",
    )(ct, st, ab)


def _dft_pos_sample(ab, ct, st):
    tr = 1024
    row_blk0 = NP_TOK // DEC_SEQ
    return pl.pallas_call(
        _dft_pos_kernel,
        grid=(DEC_BATCH, N_FGROUPS, DEC_SEQ // tr),
        in_specs=[pl.BlockSpec((tr, DEC_SEQ), lambda b, g, t: (t, 0)),
                  pl.BlockSpec((tr, DEC_SEQ), lambda b, g, t: (t, 0)),
                  pl.BlockSpec((DEC_SEQ, 2 * FGROUP_DIM), lambda b, g, t: (row_blk0 + b, g))],
        out_specs=pl.BlockSpec((tr, FGROUP_DIM), lambda b, g, t: (b * (DEC_SEQ // tr) + t, g)),
        out_shape=jax.ShapeDtypeStruct((NS_TOK, D_FOURIER), BF16),
        compiler_params=_params(3),
        name="dft_pos_sample",
    )(ct, st, ab)


def _wout_kernel(a_ref, f_ref, w_ref, x_ref, g_ref, tiled_ref, o_ref, heads_ref):
    for c in range(a_ref.shape[0] // WS_ROWS):
        rs = slice(c * WS_ROWS, (c + 1) * WS_ROWS)
        acc = (jnp.dot(a_ref[rs, :], w_ref[0:D_ATT, :], preferred_element_type=F32)
               + jnp.dot(f_ref[rs, :], w_ref[D_ATT:D_ATT + D_FOURIER, :], preferred_element_type=F32))
        o_ref[rs, :] = x_ref[rs, :] + g_ref[0] * acc

    for tt in range(tiled_ref.shape[0] // (N_HEADS * SUBLANES)):
        for s in range(SUBLANES):
            for hb in range(N_HEADS // SUBLANES):
                src = pl.ds(tt * (N_HEADS * SUBLANES) + hb * SUBLANES * SUBLANES + s, SUBLANES,
                            stride=SUBLANES)
                dst = pl.ds((tt * SUBLANES + s) * N_HEADS + hb * SUBLANES, SUBLANES)
                heads_ref[dst, :] = tiled_ref[src, :]


def _wout(att, fou, w_out, x, mod3, tiled, *, row0, name):
    assert HEAD_DIM == LANES and N_HEADS % SUBLANES == 0
    tm, tn = 1024, 1024
    n_rows = x.shape[0]
    blk0 = row0 // tm
    nj, ni = D_MODEL // tn, n_rows // tm
    flat = tiled.reshape(-1, LANES)
    slab = flat.shape[0] // (nj * ni)
    assert slab * nj * ni == flat.shape[0] and slab % (N_HEADS * SUBLANES) == 0
    return pl.pallas_call(
        _wout_kernel,
        grid=(nj, ni),
        in_specs=[pl.BlockSpec((tm, D_ATT), lambda j, i: (i, 0)),
                  pl.BlockSpec((tm, D_FOURIER), lambda j, i: (i, 0)),
                  pl.BlockSpec((D_MODEL, tn), lambda j, i: (0, j)),
                  pl.BlockSpec((tm, tn), lambda j, i: (i, j)),
                  pl.BlockSpec((1, 1, tn), lambda j, i: (_mod_row(blk0 + i, tm) * 6 + 2, 0, j)),
                  pl.BlockSpec((slab, LANES), lambda j, i: (j * ni + i, 0))],
        out_specs=[pl.BlockSpec((tm, tn), lambda j, i: (i, j)),
                   pl.BlockSpec((slab, LANES), lambda j, i: (j * ni + i, 0))],
        out_shape=[jax.ShapeDtypeStruct((n_rows, D_MODEL), F32),
                   jax.ShapeDtypeStruct(flat.shape, F32)],
        compiler_params=_params(2, VMEM_LIMIT_BIG),
        name=name,
    )(att, fou, w_out, x, mod3, flat)


FF_TN = 256


def _gate_up_kernel(h_ref, wg_ref, wu_ref, wd_ref, o_ref, wdb_ref, wgb_ref, wub_ref, *, n_real):
    j = pl.program_id(0)

    @pl.when((pl.program_id(1) == 0) & (j < n_real))
    def _():
        _cast_weight(wg_ref, wgb_ref)
        _cast_weight(wu_ref, wub_ref)

    @pl.when(j < n_real)
    def _():
        for c in range(h_ref.shape[0] // FF_ROWS):
            rs = slice(c * FF_ROWS, (c + 1) * FF_ROWS)
            h = h_ref[rs, :]
            g = jnp.dot(h, wgb_ref[...], preferred_element_type=F32)
            u = jnp.dot(h, wub_ref[...], preferred_element_type=F32)
            o_ref[rs, :] = (g * jax.nn.sigmoid(g) * u).astype(o_ref.dtype)
        wdb_ref[...] = wd_ref[...].astype(wdb_ref.dtype)

    @pl.when(j >= n_real)
    def _():
        o_ref[...] = jnp.zeros_like(o_ref)
        wdb_ref[...] = jnp.zeros_like(wdb_ref)


FF_ROWS = 512


def _gate_up(h2, w_gate, w_up, w_down):
    tm, tn = 2048, FF_TN
    n_real = D_FF // tn
    ni = N_TOK // tm
    wd_rows = D_FF // (n_real * ni)
    assert wd_rows * n_real * ni == D_FF and wd_rows % 16 == 0
    assert (D_FF_PAD - D_FF) == (D_FF_PAD // tn - n_real) * ni * wd_rows
    wmap = lambda j, i: (0, jnp.minimum(j, n_real - 1))
    return pl.pallas_call(
        functools.partial(_gate_up_kernel, n_real=n_real),
        grid=(D_FF_PAD // tn, ni),
        in_specs=[pl.BlockSpec((tm, D_MODEL), lambda j, i: (jnp.where(j < n_real, i, ni - 1), 0)),
                  pl.BlockSpec((D_MODEL, tn), wmap),
                  pl.BlockSpec((D_MODEL, tn), wmap),
                  pl.BlockSpec((wd_rows, D_MODEL),
                               lambda j, i: (jnp.minimum(j * ni + i, n_real * ni - 1), 0))],
        out_specs=[pl.BlockSpec((tm, tn), lambda j, i: (i, j)),
                   pl.BlockSpec((wd_rows, D_MODEL), lambda j, i: (j * ni + i, 0))],
        out_shape=[jax.ShapeDtypeStruct((N_TOK, D_FF_PAD), BF16),
                   jax.ShapeDtypeStruct((D_FF_PAD, D_MODEL), BF16)],
        scratch_shapes=[pltpu.VMEM((D_MODEL, tn), BF16), pltpu.VMEM((D_MODEL, tn), BF16)],
        compiler_params=_params(2),
        name="ffn_gate_up",
    )(h2, w_gate, w_up, w_down)


def _down_kernel(a_ref, w_ref, x_ref, g_ref, fg_ref, o_ref, r_ref, *, nk):
    k = pl.program_id(1)

    tk = a_ref.shape[1]
    k_last = D_FF - (nk - 1) * tk

    def accumulate(first, depth):
        for n in range(D_MODEL // DOWN_TN):
            ns = slice(n * DOWN_TN, (n + 1) * DOWN_TN)
            part = jnp.dot(a_ref[:, :depth], w_ref[:depth, ns], preferred_element_type=F32)
            if first:
                o_ref[:, ns] = part
            else:
                o_ref[:, ns] += part

    @pl.when(k == 0)
    def _():
        accumulate(True, tk)

    @pl.when((k > 0) & (k < nk - 1))
    def _():
        accumulate(False, tk)

    @pl.when(k == nk - 1)
    def _():
        accumulate(False, k_last)

    @pl.when(k == nk - 1)
    def _():
        lanes = LANES

        def residual_stats(c, carry):
            rows = pl.ds(pl.multiple_of(c * APPLY_ROWS, APPLY_ROWS), APPLY_ROWS)
            ss = jnp.zeros((APPLY_ROWS, lanes), F32)
            for j in range(D_MODEL // APPLY_COLS):
                cols = slice(j * APPLY_COLS, (j + 1) * APPLY_COLS)
                x2 = x_ref[rows, cols] + g_ref[0, :, cols] * o_ref[rows, cols]
                o_ref[rows, cols] = x2
                sq = x2 * x2
                for q in range(APPLY_COLS // lanes):
                    ss = ss + sq[:, q * lanes:(q + 1) * lanes]
            ms = jnp.sum(ss, axis=-1, keepdims=True) * (1.0 / D_MODEL)
            r_ref[rows, :] = lax.rsqrt(ms + EPS)
            return carry
        lax.fori_loop(0, o_ref.shape[0] // APPLY_ROWS, residual_stats, 0)

        def normalise(rows, cols):
            o_ref[rows, cols] = o_ref[rows, cols] * r_ref[rows, :] * fg_ref[:, cols]
        _row_col_blocks(o_ref.shape[0], o_ref.shape[1], normalise)


DOWN_TN = 512


def _down(a, wd, x1, mod3, final_g, *, row0, name):
    tm, tk = 512, 1024
    nk = D_FF_PAD // tk
    assert nk * tk == D_FF_PAD
    n_rows = x1.shape[0]
    blk0 = row0 // tm
    return pl.pallas_call(
        functools.partial(_down_kernel, nk=nk),
        grid=(n_rows // tm, nk),
        in_specs=[pl.BlockSpec((tm, tk), lambda i, k: (blk0 + i, k)),
                  pl.BlockSpec((tk, D_MODEL), lambda i, k: (k, 0)),
                  pl.BlockSpec((tm, D_MODEL), lambda i, k: (i, 0)),
                  pl.BlockSpec((1, 1, D_MODEL), lambda i, k: (_mod_row(blk0 + i, tm) * 6 + 5, 0, 0)),
                  pl.BlockSpec((1, D_MODEL), lambda i, k: (0, 0))],
        out_specs=pl.BlockSpec((tm, D_MODEL), lambda i, k: (i, 0)),
        out_shape=jax.ShapeDtypeStruct((n_rows, D_MODEL), F32),
        scratch_shapes=[pltpu.VMEM((tm, 1), F32)],
        compiler_params=_params(2),
        name=name,
    )(a, wd, x1, mod3, final_g)


def kernel(x_prompt, x_sample, cache_k, cache_v, c, c_ctx, w_ada, b_ada, norm1_g, w_in, rpb,
           w_out, norm2_g, w_gate, w_up, w_down, final_g):
    xp = x_prompt.reshape(NP_TOK, D_MODEL)
    xs = x_sample.reshape(NS_TOK, D_MODEL)

    cvec = jnp.concatenate([c_ctx[None, :], c, jnp.zeros((N_MOD - 1 - DEC_BATCH, D_MODEL), F32)], axis=0)
    mod = _ada(cvec, w_ada[0], b_ada[0][None, :])
    mod3 = mod.reshape(N_MOD * 6, 1, D_MODEL)

    h = _norm_modulate(xp, xs, norm1_g[0][None, :], mod3, shift_part=0, scale_part=1,
                       name="norm1_mod")

    w_in0 = w_in[0]
    tm, tn = 1024, 512
    n_att_blks = D_ATT // tn
    tm2 = 2 * tm
    qu = _ws_matmul(h, w_in0, row_blk0=0, n_row_blks=N_TOK // tm2,
                    col_map=lambda j: jnp.where(j < n_att_blks, j, j + 2 * n_att_blks),
                    n_col_blks=2 * n_att_blks, tm=tm2, tn=tn, out_dtype=BF16, name="w_in_q_u",
                    vmem=VMEM_LIMIT_BIG)
    kv_s = _ws_matmul(h, w_in0, row_blk0=NP_TOK // tm2, n_row_blks=NS_TOK // tm2,
                      col_map=lambda j: j + n_att_blks, n_col_blks=2 * n_att_blks,
                      tm=tm2, tn=tn, out_dtype=BF16, name="w_in_kv_sample", vmem=VMEM_LIMIT_BIG)
    newk = _ws_matmul(h, w_in0, row_blk0=0, n_row_blks=NP_TOK // tm,
                      col_map=lambda j: j + n_att_blks, n_col_blks=n_att_blks,
                      tm=tm, tn=tn, out_dtype=F32, name="w_in_k_prompt", tiled_out=True)
    newv = _ws_matmul(h, w_in0, row_blk0=0, n_row_blks=NP_TOK // tm,
                      col_map=lambda j: j + 2 * n_att_blks, n_col_blks=n_att_blks,
                      tm=tm, tn=tn, out_dtype=F32, name="w_in_v_prompt", tiled_out=True)

    cls_of_g, patterns = _nbr_classes()
    bias = _rpb_bias(rpb[0].reshape(-1), patterns)
    ck = cache_k[:, 0].reshape(DEC_BATCH * PAST_LEN * N_HEADS, HEAD_DIM)
    cv = cache_v[:, 0].reshape(DEC_BATCH * PAST_LEN * N_HEADS, HEAD_DIM)
    att_s, w_out_b = _nbr_attention(qu, kv_s, ck, cv, bias, cls_of_g, w_out[0])
    att_p = _ctx_attention(qu, newk, newv)

    cc, sc = _dft_tables(FGROUP_DIM)
    w1 = jnp.asarray(np.concatenate([cc, -sc], axis=1)).astype(BF16)
    ab = _dft_chan(qu, w1)
    ctp, stp = _dft_tables(SEQ)
    fou_p = _dft_pos_prompt(ab, jnp.asarray(ctp).astype(BF16), jnp.asarray(stp).astype(BF16))
    cts, sts = _dft_tables(DEC_SEQ)
    fou_s = _dft_pos_sample(ab, jnp.asarray(cts).astype(BF16), jnp.asarray(sts).astype(BF16))

    x1p, k_heads = _wout(att_p, fou_p, w_out_b, xp, mod3, newk, row0=0, name="w_out_prompt")
    x1s, v_heads = _wout(att_s, fou_s, w_out_b, xs, mod3, newv, row0=NP_TOK, name="w_out_sample")

    h2 = _norm_modulate(x1p, x1s, norm2_g[0][None, :], mod3, shift_part=3, scale_part=4,
                        name="norm2_mod")
    a, wd = _gate_up(h2, w_gate[0], w_up[0], w_down[0])
    fg = final_g[None, :]
    y_prompt = _down(a, wd, x1p, mod3, fg, row0=0,
                     name="ffn_down_prompt").reshape(BATCH, SEQ, D_MODEL)
    y_sample = _down(a, wd, x1s, mod3, fg, row0=NP_TOK,
                     name="ffn_down_sample").reshape(DEC_BATCH, DEC_SEQ, D_MODEL)
    new_cache_k = k_heads.reshape(BATCH, 1, SEQ, N_HEADS, HEAD_DIM)
    new_cache_v = v_heads.reshape(BATCH, 1, SEQ, N_HEADS, HEAD_DIM)
    return (y_prompt, y_sample, new_cache_k, new_cache_v)
```

```python
import functools

import numpy as np
import jax
import jax.numpy as jnp
from jax import lax
from jax.experimental import pallas as pl
from jax.experimental.pallas import tpu as pltpu

F32 = jnp.float32
BF16 = jnp.bfloat16

D_MODEL = 4096
BATCH = 32
SEQ = 256
DEC_BATCH = 4
DEC_SEQ = 2048
PAST_LEN = 256
GRID_W = 64
GRID_ROWS = DEC_SEQ // GRID_W
D_ATT = 2048
D_FOURIER = 2048
HEAD_DIM = 128
N_HEADS = 16
N_FGROUPS = 4
FGROUP_DIM = 512
WIN_H = 8
WIN_W = 16
D_FF = 11008
D_FF_PAD = 11264
EPS = 1e-6
NEG_INF = -1e30
SCALE = HEAD_DIM ** -0.5
LOG2E = 1.4426950408889634
SCALE2 = SCALE * LOG2E

NP_TOK = BATCH * SEQ
NS_TOK = DEC_BATCH * DEC_SEQ
N_TOK = NP_TOK + NS_TOK
N_MOD = 8

VMEM_LIMIT = 56 * 1024 * 1024
VMEM_LIMIT_BIG = 60 * 1024 * 1024


def _params(n_axes, vmem=VMEM_LIMIT):
    return pltpu.CompilerParams(dimension_semantics=("arbitrary",) * n_axes,
                                vmem_limit_bytes=vmem)


def _mod_row(i, tm):
    nh = NP_TOK // tm
    return jnp.where(i < nh, 0, 1 + (i - nh) // (DEC_SEQ // tm))


def _ada_kernel(c_ref, w_ref, b_ref, o_ref):
    c = c_ref[...]
    s = (c * jax.nn.sigmoid(c)).astype(BF16)
    o_ref[...] = jnp.dot(s, w_ref[...].astype(BF16), preferred_element_type=F32) + b_ref[...]


def _ada(cvec, w_ada, b_ada):
    tn = 1024
    n = w_ada.shape[1]
    return pl.pallas_call(
        _ada_kernel,
        grid=(n // tn,),
        in_specs=[pl.BlockSpec((N_MOD, D_MODEL), lambda j: (0, 0)),
                  pl.BlockSpec((D_MODEL, tn), lambda j: (0, j)),
                  pl.BlockSpec((1, tn), lambda j: (0, j))],
        out_specs=pl.BlockSpec((N_MOD, tn), lambda j: (0, j)),
        out_shape=jax.ShapeDtypeStruct((N_MOD, n), F32),
        compiler_params=_params(1),
        name="ada_mod",
    )(cvec, w_ada, b_ada)


ROW_CHUNK = 8


STATS_UNROLL = 16
APPLY_ROWS = 64
APPLY_COLS = 512


def _row_chunks(n_rows, fn):
    def body(c, carry):
        fn(pl.ds(pl.multiple_of(c * ROW_CHUNK, ROW_CHUNK), ROW_CHUNK))
        return carry
    lax.fori_loop(0, n_rows // ROW_CHUNK, body, 0, unroll=STATS_UNROLL)


def _row_col_blocks(n_rows, n_cols, fn):
    def body(c, carry):
        rows = pl.ds(pl.multiple_of(c * APPLY_ROWS, APPLY_ROWS), APPLY_ROWS)
        for j in range(n_cols // APPLY_COLS):
            fn(rows, slice(j * APPLY_COLS, (j + 1) * APPLY_COLS))
        return carry
    lax.fori_loop(0, n_rows // APPLY_ROWS, body, 0)


def _norm_mod(x_ref, g_ref, sh_ref, sc_ref, o_ref, r_ref, gs_ref):
    gs_ref[...] = g_ref[...] * (1.0 + sc_ref[0])

    def stats(rows):
        x = x_ref[rows, :]
        r_ref[rows, :] = lax.rsqrt(jnp.mean(x * x, axis=-1, keepdims=True) + EPS)
    _row_chunks(o_ref.shape[0], stats)

    def apply(rows, cols):
        o_ref[rows, cols] = (x_ref[rows, cols] * r_ref[rows, :] * gs_ref[:, cols]
                             + sh_ref[0, :, cols]).astype(o_ref.dtype)
    _row_col_blocks(o_ref.shape[0], o_ref.shape[1], apply)


def _norm_mod2_kernel(xp_ref, xs_ref, g_ref, sh_ref, sc_ref, o_ref, r_ref, gs_ref, *, nh):
    i = pl.program_id(0)

    @pl.when(i < nh)
    def _():
        _norm_mod(xp_ref, g_ref, sh_ref, sc_ref, o_ref, r_ref, gs_ref)

    @pl.when(i >= nh)
    def _():
        _norm_mod(xs_ref, g_ref, sh_ref, sc_ref, o_ref, r_ref, gs_ref)


def _mod_spec(part, tm):
    return pl.BlockSpec((1, 1, D_MODEL), lambda i: (_mod_row(i, tm) * 6 + part, 0, 0))


def _norm_modulate(xp, xs, g, mod3, *, shift_part, scale_part, name):
    tm = 512
    nh = NP_TOK // tm
    return pl.pallas_call(
        functools.partial(_norm_mod2_kernel, nh=nh),
        grid=(N_TOK // tm,),
        in_specs=[pl.BlockSpec((tm, D_MODEL), lambda i: (jnp.minimum(i, nh - 1), 0)),
                  pl.BlockSpec((tm, D_MODEL), lambda i: (jnp.maximum(i - nh, 0), 0)),
                  pl.BlockSpec((1, D_MODEL), lambda i: (0, 0)),
                  _mod_spec(shift_part, tm), _mod_spec(scale_part, tm)],
        out_specs=pl.BlockSpec((tm, D_MODEL), lambda i: (i, 0)),
        out_shape=jax.ShapeDtypeStruct((N_TOK, D_MODEL), BF16),
        scratch_shapes=[pltpu.VMEM((tm, 1), F32), pltpu.VMEM((1, D_MODEL), F32)],
        compiler_params=_params(1),
        name=name,
    )(xp, xs, g, mod3, mod3)


CAST_ROWS = 512


def _cast_weight(w_ref, wb_ref):
    rows = min(CAST_ROWS, w_ref.shape[0])

    def body(c, carry):
        r = pl.multiple_of(c * rows, rows)
        wb_ref[pl.ds(r, rows), :] = w_ref[pl.ds(r, rows), :].astype(BF16)
        return carry
    lax.fori_loop(0, w_ref.shape[0] // rows, body, 0)


WS_ROWS = 512


def _ws_kernel(x_ref, w_ref, o_ref, wb_ref):
    @pl.when(pl.program_id(1) == 0)
    def _():
        _cast_weight(w_ref, wb_ref)

    for c in range(x_ref.shape[0] // WS_ROWS):
        rs = slice(c * WS_ROWS, (c + 1) * WS_ROWS)
        res = jnp.dot(x_ref[rs, :], wb_ref[...], preferred_element_type=F32).astype(o_ref.dtype)
        if len(o_ref.shape) == 2:
            o_ref[rs, :] = res
        else:
            ts = slice(c * WS_ROWS // SUBLANES, (c + 1) * WS_ROWS // SUBLANES)
            for t in range(o_ref.shape[1]):
                o_ref[ts, t] = res[:, t * LANES:(t + 1) * LANES].reshape(WS_ROWS // SUBLANES, SUBLANES, LANES)


SUBLANES, LANES = 8, 128


def _ws_matmul(x, w, *, row_blk0, n_row_blks, col_map, n_col_blks, tm, tn, out_dtype, name,
               vmem=VMEM_LIMIT, tiled_out=False):
    k = x.shape[1]
    if tiled_out:
        out_spec = pl.BlockSpec((tm // SUBLANES, tn // LANES, SUBLANES, LANES), lambda j, i: (i, j, 0, 0))
        out_shape = (n_row_blks * tm // SUBLANES, n_col_blks * tn // LANES, SUBLANES, LANES)
    else:
        out_spec = pl.BlockSpec((tm, tn), lambda j, i: (i, j))
        out_shape = (n_row_blks * tm, n_col_blks * tn)
    return pl.pallas_call(
        _ws_kernel,
        grid=(n_col_blks, n_row_blks),
        in_specs=[pl.BlockSpec((tm, k), lambda j, i: (row_blk0 + i, 0)),
                  pl.BlockSpec((k, tn), lambda j, i: (0, col_map(j)))],
        out_specs=out_spec,
        out_shape=jax.ShapeDtypeStruct(out_shape, out_dtype),
        scratch_shapes=[pltpu.VMEM((k, tn), BF16)],
        compiler_params=_params(2, vmem),
        name=name,
    )(x, w)


def _ctx_attn_kernel(q_ref, kv_ref, o_ref):
    for h in range(N_HEADS):
        hs = slice(h * HEAD_DIM, (h + 1) * HEAD_DIM)
        q = q_ref[:, hs]
        k = kv_ref[:, h].reshape(SEQ, HEAD_DIM).astype(BF16)
        v = kv_ref[:, N_HEADS + h].reshape(SEQ, HEAD_DIM).astype(BF16)
        s = lax.dot_general(q, k, (((1,), (1,)), ((), ())), preferred_element_type=F32) * SCALE2
        m = jnp.max(s, axis=-1, keepdims=True)
        p = jnp.exp2(s - m)
        l = jnp.sum(p, axis=-1, keepdims=True)
        o = jnp.dot(p.astype(BF16), v, preferred_element_type=F32) / l
        o_ref[:, hs] = o.astype(o_ref.dtype)


def _ctx_attention(qu, new_kv):
    return pl.pallas_call(
        _ctx_attn_kernel,
        grid=(BATCH,),
        in_specs=[pl.BlockSpec((SEQ, D_ATT), lambda b: (b, 0)),
                  pl.BlockSpec((SEQ // SUBLANES, 2 * N_HEADS, SUBLANES, LANES), lambda b: (b, 0, 0, 0))],
        out_specs=pl.BlockSpec((SEQ, D_ATT), lambda b: (b, 0)),
        out_shape=jax.ShapeDtypeStruct((NP_TOK, D_ATT), BF16),
        compiler_params=_params(1),
        name="ctx_attention",
    )(qu, new_kv)


NA_QROWS = 4
NA_KROWS = NA_QROWS + WIN_H
NA_Q = NA_QROWS * GRID_W
NA_K = NA_KROWS * GRID_W
NA_GROUPS = GRID_ROWS // NA_QROWS


def _nbr_window_start(g):
    lo, hi = 0, GRID_ROWS - NA_KROWS
    if isinstance(g, int):
        return min(max(NA_QROWS * g - WIN_H // 2, lo), hi)
    return jnp.clip(NA_QROWS * g - WIN_H // 2, lo, hi)


def _nbr_classes():
    patterns, cls_of_g = [], []
    for g in range(NA_GROUPS):
        start = _nbr_window_start(g)
        pat = []
        for i in range(NA_QROWS):
            r = NA_QROWS * g + i
            rstart = min(max(r - WIN_H // 2, 0), GRID_ROWS - WIN_H)
            pat.append(tuple((start + j - r + WIN_H - 1) if rstart <= start + j < rstart + WIN_H else None
                             for j in range(NA_KROWS)))
        pat = tuple(pat)
        if pat not in patterns:
            patterns.append(pat)
        cls_of_g.append(patterns.index(pat))
    return tuple(cls_of_g), tuple(patterns)


def _rpb_bias_kernel(rpb_ref, o_ref, t_ref, *, patterns):
    h = pl.program_id(0)
    qc = lax.broadcasted_iota(jnp.int32, (GRID_W, GRID_W), 0)
    kc = lax.broadcasted_iota(jnp.int32, (GRID_W, GRID_W), 1)
    dc = jnp.clip(kc - qc + (WIN_W - 1), 0, 2 * WIN_W - 2)
    cstart = jnp.clip(qc - WIN_W // 2, 0, GRID_W - WIN_W)
    mask = (kc >= cstart) & (kc < cstart + WIN_W)
    n_dc = 2 * WIN_W - 1
    n_dr = 2 * WIN_H - 1
    for dr in range(n_dr):
        t = jnp.zeros((GRID_W, GRID_W), F32)
        for d in range(n_dc):
            t = jnp.where(dc == d, rpb_ref[h * (n_dr * n_dc) + dr * n_dc + d], t)
        t_ref[dr] = jnp.where(mask, t * LOG2E, NEG_INF)
    outside = jnp.full((GRID_W, GRID_W), NEG_INF, F32)
    for c, pat in enumerate(patterns):
        for i in range(NA_QROWS):
            for j in range(NA_KROWS):
                dr = pat[i][j]
                o_ref[c, 0, i * GRID_W:(i + 1) * GRID_W, j * GRID_W:(j + 1) * GRID_W] = (
                    outside if dr is None else t_ref[dr])


def _rpb_bias(rpb_flat, patterns):
    n_cls = len(patterns)
    return pl.pallas_call(
        functools.partial(_rpb_bias_kernel, patterns=patterns),
        grid=(N_HEADS,),
        in_specs=[pl.BlockSpec(memory_space=pltpu.SMEM)],
        out_specs=pl.BlockSpec((n_cls, 1, NA_Q, NA_K), lambda h: (0, h, 0, 0)),
        out_shape=jax.ShapeDtypeStruct((n_cls, N_HEADS, NA_Q, NA_K), F32),
        scratch_shapes=[pltpu.VMEM((2 * WIN_H - 1, GRID_W, GRID_W), F32)],
        compiler_params=_params(1),
        name="rpb_bias",
    )(rpb_flat)


NA_HEADS = 4
NA_COLS = NA_HEADS * HEAD_DIM


def _nbr_attn_kernel(q_ref, k_ref, v_ref, kc_ref, vc_ref, bias_ref, w_ref, o_ref, wb_ref,
                     kcb_ref, vcb_ref, *, cls_of_g):
    head0 = pl.program_id(0) * NA_HEADS
    for h in range(NA_HEADS):
        hs = slice(h * HEAD_DIM, (h + 1) * HEAD_DIM)
        rows = pl.ds(head0 + h, PAST_LEN, stride=N_HEADS)
        kcb_ref[:, hs] = kc_ref[rows, :].astype(BF16)
        vcb_ref[:, hs] = vc_ref[rows, :].astype(BF16)
    _cast_weight(w_ref, wb_ref)
    dn = (((1,), (1,)), ((), ()))

    def group_body(g, carry):
        cls = jnp.int32(cls_of_g[0])
        for gg in range(1, NA_GROUPS):
            if cls_of_g[gg] != cls_of_g[gg - 1]:
                cls = jnp.where(g >= gg, cls_of_g[gg], cls)
        q0 = pl.multiple_of(g * NA_Q, NA_Q)
        k0 = pl.multiple_of(_nbr_window_start(g) * GRID_W, GRID_W)
        for h in range(NA_HEADS):
            hs = slice(h * HEAD_DIM, (h + 1) * HEAD_DIM)
            q = q_ref[pl.ds(q0, NA_Q), hs]
            kw = k_ref[pl.ds(k0, NA_K), hs]
            vw = v_ref[pl.ds(k0, NA_K), hs]
            s_loc = lax.dot_general(q, kw, dn, preferred_element_type=F32) * SCALE2 + bias_ref[cls, h]
            s_ctx = lax.dot_general(q, kcb_ref[:, hs], dn, preferred_element_type=F32) * SCALE2
            m = jnp.maximum(jnp.max(s_loc, axis=-1, keepdims=True),
                            jnp.max(s_ctx, axis=-1, keepdims=True))
            p_loc = jnp.exp2(s_loc - m)
            p_ctx = jnp.exp2(s_ctx - m)
            l = jnp.sum(p_loc, axis=-1, keepdims=True) + jnp.sum(p_ctx, axis=-1, keepdims=True)
            o = (jnp.dot(p_loc.astype(BF16), vw, preferred_element_type=F32)
                 + jnp.dot(p_ctx.astype(BF16), vcb_ref[:, hs], preferred_element_type=F32)) / l
            o_ref[pl.ds(q0, NA_Q), hs] = o.astype(o_ref.dtype)
        return carry

    lax.fori_loop(0, NA_GROUPS, group_body, 0)


def _nbr_attention(qu, kv_s, ck, cv, bias, cls_of_g, w_out):
    n_hg = N_HEADS // NA_HEADS
    n_cls = bias.shape[0]
    row_blk0 = NP_TOK // DEC_SEQ
    slab = w_out.shape[0] // (n_hg * DEC_BATCH)
    assert slab * n_hg * DEC_BATCH == w_out.shape[0] and slab % 16 == 0
    return pl.pallas_call(
        functools.partial(_nbr_attn_kernel, cls_of_g=cls_of_g),
        grid=(n_hg, DEC_BATCH),
        in_specs=[pl.BlockSpec((DEC_SEQ, NA_COLS), lambda g, b: (row_blk0 + b, g)),
                  pl.BlockSpec((DEC_SEQ, NA_COLS), lambda g, b: (b, g)),
                  pl.BlockSpec((DEC_SEQ, NA_COLS), lambda g, b: (b, n_hg + g)),
                  pl.BlockSpec((PAST_LEN * N_HEADS, HEAD_DIM), lambda g, b: (b, 0)),
                  pl.BlockSpec((PAST_LEN * N_HEADS, HEAD_DIM), lambda g, b: (b, 0)),
                  pl.BlockSpec((n_cls, NA_HEADS, NA_Q, NA_K), lambda g, b: (0, g, 0, 0)),
                  pl.BlockSpec((slab, w_out.shape[1]), lambda g, b: (g * DEC_BATCH + b, 0))],
        out_specs=[pl.BlockSpec((DEC_SEQ, NA_COLS), lambda g, b: (b, g)),
                   pl.BlockSpec((slab, w_out.shape[1]), lambda g, b: (g * DEC_BATCH + b, 0))],
        out_shape=[jax.ShapeDtypeStruct((NS_TOK, D_ATT), BF16),
                   jax.ShapeDtypeStruct(w_out.shape, BF16)],
        scratch_shapes=[pltpu.VMEM((PAST_LEN, NA_COLS), BF16),
                        pltpu.VMEM((PAST_LEN, NA_COLS), BF16)],
        compiler_params=_params(2, VMEM_LIMIT_BIG),
        name="nbr_attention",
    )(qu, kv_s, kv_s, ck, cv, bias, w_out)


def _dft_tables(n):
    idx = np.arange(n, dtype=np.int64)
    ang = (2.0 * np.pi / n) * ((idx[:, None] * idx[None, :]) % n).astype(np.float64)
    return ((np.cos(ang) / np.sqrt(n)).astype(np.float32),
            (-np.sin(ang) / np.sqrt(n)).astype(np.float32))


def _dft_chan_kernel(u_ref, w_ref, o_ref):
    for c in range(u_ref.shape[0] // WS_ROWS):
        rs = slice(c * WS_ROWS, (c + 1) * WS_ROWS)
        for g in range(N_FGROUPS):
            u = u_ref[rs, g * FGROUP_DIM:(g + 1) * FGROUP_DIM]
            o_ref[rs, g * 2 * FGROUP_DIM:(g + 1) * 2 * FGROUP_DIM] = jnp.dot(
                u, w_ref[...], preferred_element_type=F32).astype(o_ref.dtype)


def _dft_chan(qu, w1):
    tm = 2048
    return pl.pallas_call(
        _dft_chan_kernel,
        grid=(N_TOK // tm,),
        in_specs=[pl.BlockSpec((tm, D_FOURIER), lambda i: (i, 1)),
                  pl.BlockSpec((FGROUP_DIM, 2 * FGROUP_DIM), lambda i: (0, 0))],
        out_specs=pl.BlockSpec((tm, 2 * D_FOURIER), lambda i: (i, 0)),
        out_shape=jax.ShapeDtypeStruct((N_TOK, 2 * D_FOURIER), BF16),
        compiler_params=_params(1),
        name="dft_channels",
    )(qu, w1)


def _dft_pos_kernel(ct_ref, st_ref, ab_ref, o_ref):
    n_in = ct_ref.shape[1]
    n_out = ct_ref.shape[0]
    for s in range(ab_ref.shape[0] // n_in):
        for g in range(ab_ref.shape[1] // (2 * FGROUP_DIM)):
            rows = slice(s * n_in, (s + 1) * n_in)
            a = ab_ref[rows, g * 2 * FGROUP_DIM:g * 2 * FGROUP_DIM + FGROUP_DIM]
            b = ab_ref[rows, g * 2 * FGROUP_DIM + FGROUP_DIM:(g + 1) * 2 * FGROUP_DIM]
            o = (jnp.dot(ct_ref[...], a, preferred_element_type=F32)
                 + jnp.dot(st_ref[...], b, preferred_element_type=F32))
            o_ref[s * n_out:(s + 1) * n_out, g * FGROUP_DIM:(g + 1) * FGROUP_DIM] = o.astype(o_ref.dtype)


def _dft_pos_prompt(ab, ct, st):
    n_seq = 4
    rows = n_seq * SEQ
    return pl.pallas_call(
        _dft_pos_kernel,
        grid=(BATCH // n_seq,),
        in_specs=[pl.BlockSpec((SEQ, SEQ), lambda b: (0, 0)),
                  pl.BlockSpec((SEQ, SEQ), lambda b: (0, 0)),
                  pl.BlockSpec((rows, 2 * D_FOURIER), lambda b: (b, 0))],
        out_specs=pl.BlockSpec((rows, D_FOURIER), lambda b: (b, 0)),
        out_shape=jax.ShapeDtypeStruct((NP_TOK, D_FOURIER), BF16),
        compiler_params=_params(1),
        name="dft_pos_prompt",
    )(ct, st, ab)


def _dft_pos_sample(ab, ct, st):
    tr = 1024
    row_blk0 = NP_TOK // DEC_SEQ
    return pl.pallas_call(
        _dft_pos_kernel,
        grid=(DEC_BATCH, N_FGROUPS, DEC_SEQ // tr),
        in_specs=[pl.BlockSpec((tr, DEC_SEQ), lambda b, g, t: (t, 0)),
                  pl.BlockSpec((tr, DEC_SEQ), lambda b, g, t: (t, 0)),
                  pl.BlockSpec((DEC_SEQ, 2 * FGROUP_DIM), lambda b, g, t: (row_blk0 + b, g))],
        out_specs=pl.BlockSpec((tr, FGROUP_DIM), lambda b, g, t: (b * (DEC_SEQ // tr) + t, g)),
        out_shape=jax.ShapeDtypeStruct((NS_TOK, D_FOURIER), BF16),
        compiler_params=_params(3),
        name="dft_pos_sample",
    )(ct, st, ab)


def _wout_kernel(a_ref, f_ref, w_ref, x_ref, g_ref, tiled_ref, o_ref, heads_ref):
    for c in range(a_ref.shape[0] // WS_ROWS):
        rs = slice(c * WS_ROWS, (c + 1) * WS_ROWS)
        acc = (jnp.dot(a_ref[rs, :], w_ref[0:D_ATT, :], preferred_element_type=F32)
               + jnp.dot(f_ref[rs, :], w_ref[D_ATT:D_ATT + D_FOURIER, :], preferred_element_type=F32))
        o_ref[rs, :] = x_ref[rs, :] + g_ref[0] * acc

    for tt in range(tiled_ref.shape[0]):
        for s in range(SUBLANES):
            for hb in range(N_HEADS // SUBLANES):
                src = pl.ds(hb * SUBLANES * SUBLANES + s, SUBLANES, stride=SUBLANES)
                dst = pl.ds((tt * SUBLANES + s) * N_HEADS + hb * SUBLANES, SUBLANES)
                heads_ref[dst, :] = tiled_ref[tt, src, :]


def _wout(att, fou, w_out, x, mod3, kv_tiled, half, *, row0, name):
    assert HEAD_DIM == LANES and N_HEADS % SUBLANES == 0
    tm, tn = 1024, 1024
    n_rows = x.shape[0]
    blk0 = row0 // tm
    nj, ni = D_MODEL // tn, n_rows // tm
    n_tiles = kv_tiled.shape[0]
    view = kv_tiled.reshape(n_tiles, 2, N_HEADS * SUBLANES, LANES)
    tiles = n_tiles // (nj * ni)
    assert tiles * nj * ni == n_tiles
    slab = tiles * N_HEADS * SUBLANES
    return pl.pallas_call(
        _wout_kernel,
        grid=(nj, ni),
        in_specs=[pl.BlockSpec((tm, D_ATT), lambda j, i: (i, 0)),
                  pl.BlockSpec((tm, D_FOURIER), lambda j, i: (i, 0)),
                  pl.BlockSpec((D_MODEL, tn), lambda j, i: (0, j)),
                  pl.BlockSpec((tm, tn), lambda j, i: (i, j)),
                  pl.BlockSpec((1, 1, tn), lambda j, i: (_mod_row(blk0 + i, tm) * 6 + 2, 0, j)),
                  pl.BlockSpec((tiles, None, N_HEADS * SUBLANES, LANES),
                               lambda j, i: (j * ni + i, half, 0, 0))],
        out_specs=[pl.BlockSpec((tm, tn), lambda j, i: (i, j)),
                   pl.BlockSpec((slab, LANES), lambda j, i: (j * ni + i, 0))],
        out_shape=[jax.ShapeDtypeStruct((n_rows, D_MODEL), F32),
                   jax.ShapeDtypeStruct((n_tiles * N_HEADS * SUBLANES, LANES), F32)],
        compiler_params=_params(2, VMEM_LIMIT_BIG),
        name=name,
    )(att, fou, w_out, x, mod3, view)


FF_TN = 256


def _gate_up_kernel(h_ref, wg_ref, wu_ref, wd_ref, o_ref, wdb_ref, wgb_ref, wub_ref, *, n_real):
    j = pl.program_id(0)

    @pl.when((pl.program_id(1) == 0) & (j < n_real))
    def _():
        _cast_weight(wg_ref, wgb_ref)
        _cast_weight(wu_ref, wub_ref)

    @pl.when(j < n_real)
    def _():
        for c in range(h_ref.shape[0] // FF_ROWS):
            rs = slice(c * FF_ROWS, (c + 1) * FF_ROWS)
            h = h_ref[rs, :]
            g = jnp.dot(h, wgb_ref[...], preferred_element_type=F32)
            u = jnp.dot(h, wub_ref[...], preferred_element_type=F32)
            o_ref[rs, :] = (g * jax.nn.sigmoid(g) * u).astype(o_ref.dtype)
        wdb_ref[...] = wd_ref[...].astype(wdb_ref.dtype)

    @pl.when(j >= n_real)
    def _():
        o_ref[...] = jnp.zeros_like(o_ref)
        wdb_ref[...] = jnp.zeros_like(wdb_ref)


FF_ROWS = 512


def _gate_up(h2, w_gate, w_up, w_down):
    tm, tn = 2048, FF_TN
    n_real = D_FF // tn
    ni = N_TOK // tm
    wd_rows = D_FF // (n_real * ni)
    assert wd_rows * n_real * ni == D_FF and wd_rows % 16 == 0
    assert (D_FF_PAD - D_FF) == (D_FF_PAD // tn - n_real) * ni * wd_rows
    wmap = lambda j, i: (0, jnp.minimum(j, n_real - 1))
    return pl.pallas_call(
        functools.partial(_gate_up_kernel, n_real=n_real),
        grid=(D_FF_PAD // tn, ni),
        in_specs=[pl.BlockSpec((tm, D_MODEL), lambda j, i: (jnp.where(j < n_real, i, ni - 1), 0)),
                  pl.BlockSpec((D_MODEL, tn), wmap),
                  pl.BlockSpec((D_MODEL, tn), wmap),
                  pl.BlockSpec((wd_rows, D_MODEL),
                               lambda j, i: (jnp.minimum(j * ni + i, n_real * ni - 1), 0))],
        out_specs=[pl.BlockSpec((tm, tn), lambda j, i: (i, j)),
                   pl.BlockSpec((wd_rows, D_MODEL), lambda j, i: (j * ni + i, 0))],
        out_shape=[jax.ShapeDtypeStruct((N_TOK, D_FF_PAD), BF16),
                   jax.ShapeDtypeStruct((D_FF_PAD, D_MODEL), BF16)],
        scratch_shapes=[pltpu.VMEM((D_MODEL, tn), BF16), pltpu.VMEM((D_MODEL, tn), BF16)],
        compiler_params=_params(2),
        name="ffn_gate_up",
    )(h2, w_gate, w_up, w_down)


def _down_kernel(a_ref, w_ref, x_ref, g_ref, fg_ref, o_ref, r_ref, *, nk):
    k = pl.program_id(1)

    tk = a_ref.shape[1]
    k_last = D_FF - (nk - 1) * tk

    def accumulate(first, depth):
        for n in range(D_MODEL // DOWN_TN):
            ns = slice(n * DOWN_TN, (n + 1) * DOWN_TN)
            part = jnp.dot(a_ref[:, :depth], w_ref[:depth, ns], preferred_element_type=F32)
            if first:
                o_ref[:, ns] = part
            else:
                o_ref[:, ns] += part

    @pl.when(k == 0)
    def _():
        accumulate(True, tk)

    @pl.when((k > 0) & (k < nk - 1))
    def _():
        accumulate(False, tk)

    @pl.when(k == nk - 1)
    def _():
        accumulate(False, k_last)

    @pl.when(k == nk - 1)
    def _():
        lanes = LANES

        def residual_stats(c, carry):
            rows = pl.ds(pl.multiple_of(c * APPLY_ROWS, APPLY_ROWS), APPLY_ROWS)
            ss = jnp.zeros((APPLY_ROWS, lanes), F32)
            for j in range(D_MODEL // APPLY_COLS):
                cols = slice(j * APPLY_COLS, (j + 1) * APPLY_COLS)
                x2 = x_ref[rows, cols] + g_ref[0, :, cols] * o_ref[rows, cols]
                o_ref[rows, cols] = x2
                sq = x2 * x2
                for q in range(APPLY_COLS // lanes):
                    ss = ss + sq[:, q * lanes:(q + 1) * lanes]
            ms = jnp.sum(ss, axis=-1, keepdims=True) * (1.0 / D_MODEL)
            r_ref[rows, :] = lax.rsqrt(ms + EPS)
            return carry
        lax.fori_loop(0, o_ref.shape[0] // APPLY_ROWS, residual_stats, 0)

        def normalise(rows, cols):
            o_ref[rows, cols] = o_ref[rows, cols] * r_ref[rows, :] * fg_ref[:, cols]
        _row_col_blocks(o_ref.shape[0], o_ref.shape[1], normalise)


DOWN_TN = 512


def _down(a, wd, x1, mod3, final_g, *, row0, name):
    tm, tk = 512, 1024
    nk = D_FF_PAD // tk
    assert nk * tk == D_FF_PAD
    n_rows = x1.shape[0]
    blk0 = row0 // tm
    return pl.pallas_call(
        functools.partial(_down_kernel, nk=nk),
        grid=(n_rows // tm, nk),
        in_specs=[pl.BlockSpec((tm, tk), lambda i, k: (blk0 + i, k)),
                  pl.BlockSpec((tk, D_MODEL), lambda i, k: (k, 0)),
                  pl.BlockSpec((tm, D_MODEL), lambda i, k: (i, 0)),
                  pl.BlockSpec((1, 1, D_MODEL), lambda i, k: (_mod_row(blk0 + i, tm) * 6 + 5, 0, 0)),
                  pl.BlockSpec((1, D_MODEL), lambda i, k: (0, 0))],
        out_specs=pl.BlockSpec((tm, D_MODEL), lambda i, k: (i, 0)),
        out_shape=jax.ShapeDtypeStruct((n_rows, D_MODEL), F32),
        scratch_shapes=[pltpu.VMEM((tm, 1), F32)],
        compiler_params=_params(2),
        name=name,
    )(a, wd, x1, mod3, final_g)


def kernel(x_prompt, x_sample, cache_k, cache_v, c, c_ctx, w_ada, b_ada, norm1_g, w_in, rpb,
           w_out, norm2_g, w_gate, w_up, w_down, final_g):
    xp = x_prompt.reshape(NP_TOK, D_MODEL)
    xs = x_sample.reshape(NS_TOK, D_MODEL)

    cvec = jnp.concatenate([c_ctx[None, :], c, jnp.zeros((N_MOD - 1 - DEC_BATCH, D_MODEL), F32)], axis=0)
    mod = _ada(cvec, w_ada[0], b_ada[0][None, :])
    mod3 = mod.reshape(N_MOD * 6, 1, D_MODEL)

    h = _norm_modulate(xp, xs, norm1_g[0][None, :], mod3, shift_part=0, scale_part=1,
                       name="norm1_mod")

    w_in0 = w_in[0]
    tm, tn = 1024, 512
    n_att_blks = D_ATT // tn
    tm2 = 2 * tm
    qu = _ws_matmul(h, w_in0, row_blk0=0, n_row_blks=N_TOK // tm2,
                    col_map=lambda j: jnp.where(j < n_att_blks, j, j + 2 * n_att_blks),
                    n_col_blks=2 * n_att_blks, tm=tm2, tn=tn, out_dtype=BF16, name="w_in_q_u",
                    vmem=VMEM_LIMIT_BIG)
    kv_s = _ws_matmul(h, w_in0, row_blk0=NP_TOK // tm2, n_row_blks=NS_TOK // tm2,
                      col_map=lambda j: j + n_att_blks, n_col_blks=2 * n_att_blks,
                      tm=tm2, tn=tn, out_dtype=BF16, name="w_in_kv_sample", vmem=VMEM_LIMIT_BIG)
    new_kv = _ws_matmul(h, w_in0, row_blk0=0, n_row_blks=NP_TOK // tm,
                        col_map=lambda j: j + n_att_blks, n_col_blks=2 * n_att_blks,
                        tm=tm, tn=tn, out_dtype=F32, name="w_in_kv_prompt", tiled_out=True)

    cls_of_g, patterns = _nbr_classes()
    bias = _rpb_bias(rpb[0].reshape(-1), patterns)
    ck = cache_k[:, 0].reshape(DEC_BATCH * PAST_LEN * N_HEADS, HEAD_DIM)
    cv = cache_v[:, 0].reshape(DEC_BATCH * PAST_LEN * N_HEADS, HEAD_DIM)
    att_s, w_out_b = _nbr_attention(qu, kv_s, ck, cv, bias, cls_of_g, w_out[0])
    att_p = _ctx_attention(qu, new_kv)

    cc, sc = _dft_tables(FGROUP_DIM)
    w1 = jnp.asarray(np.concatenate([cc, -sc], axis=1)).astype(BF16)
    ab = _dft_chan(qu, w1)
    ctp, stp = _dft_tables(SEQ)
    fou_p = _dft_pos_prompt(ab, jnp.asarray(ctp).astype(BF16), jnp.asarray(stp).astype(BF16))
    cts, sts = _dft_tables(DEC_SEQ)
    fou_s = _dft_pos_sample(ab, jnp.asarray(cts).astype(BF16), jnp.asarray(sts).astype(BF16))

    x1p, k_heads = _wout(att_p, fou_p, w_out_b, xp, mod3, new_kv, 0, row0=0, name="w_out_prompt")
    x1s, v_heads = _wout(att_s, fou_s, w_out_b, xs, mod3, new_kv, 1, row0=NP_TOK, name="w_out_sample")

    h2 = _norm_modulate(x1p, x1s, norm2_g[0][None, :], mod3, shift_part=3, scale_part=4,
                        name="norm2_mod")
    a, wd = _gate_up(h2, w_gate[0], w_up[0], w_down[0])
    fg = final_g[None, :]
    y_prompt = _down(a, wd, x1p, mod3, fg, row0=0,
                     name="ffn_down_prompt").reshape(BATCH, SEQ, D_MODEL)
    y_sample = _down(a, wd, x1s, mod3, fg, row0=NP_TOK,
                     name="ffn_down_sample").reshape(DEC_BATCH, DEC_SEQ, D_MODEL)
    new_cache_k = k_heads.reshape(BATCH, 1, SEQ, N_HEADS, HEAD_DIM)
    new_cache_v = v_heads.reshape(BATCH, 1, SEQ, N_HEADS, HEAD_DIM)
    return (y_prompt, y_sample, new_cache_k, new_cache_v)
```
